```python
import numpy as np
import jax
import jax.numpy as jnp
from jax import lax

D_MODEL = 4096
BATCH = 32
SEQ = 256
DEPTH = 2
DEC_BATCH = 4
DEC_SEQ = 4096
PAST_LEN = 512

GRID_W = 64
HEAD_DIM = 128
MIX_W = D_MODEL
W_A = MIX_W // 4
W_B = MIX_W // 4
W_C = MIX_W // 4
W_D = MIX_W - W_A - W_B - W_C
H_A = W_A // HEAD_DIM
H_B = W_B // HEAD_DIM
KV_B = max(1, H_B // 4)
G_B = H_B // KV_B
NA_ROWS = 8
NA_COLS = 16
BAND = 128
QBLK = 128
N_POOL = 4
POOL_WINDOWS = (2, 4, 8, 16)
POOL_GW = W_C // N_POOL
CONV_K = 31
ROPE_BASE = 10000.0
EPS = 1e-6
NEG = -1e30
IN_SPLITS = (W_A, W_A, W_A, W_A, W_B, KV_B * HEAD_DIM, KV_B * HEAD_DIM, W_B, W_C, W_C, 2 * W_D, W_D)
IN_W = 4 * W_A + 2 * W_B + 2 * KV_B * HEAD_DIM + 2 * W_C + 3 * W_D

kernel_name = 'hybrid_flow_natten_swa_pool_conformer_step'


def rmsnorm(x, g):
    x32 = x.astype(jnp.float32)
    y = x32 * lax.rsqrt(jnp.mean(x32 * x32, axis=-1, keepdims=True) + EPS)
    return (y * g.astype(jnp.float32)).astype(x.dtype)


def layernorm(x, g, b):
    x32 = x.astype(jnp.float32)
    mu = jnp.mean(x32, axis=-1, keepdims=True)
    var = jnp.mean(jnp.square(x32 - mu), axis=-1, keepdims=True)
    y = (x32 - mu) * lax.rsqrt(var + EPS) * g.astype(jnp.float32) + b.astype(jnp.float32)
    return y.astype(x.dtype)


def axial_rope(x):
    t = jnp.arange(x.shape[1])
    half = x.shape[-1] // 2
    inv = 1.0 / (ROPE_BASE ** (jnp.arange(0, half, 2, dtype=jnp.float32) / half))

    def rot(xp, pos):
        ang = pos.astype(jnp.float32)[:, None] * inv[None, :]
        cos = jnp.cos(ang)[None, :, None, :]
        sin = jnp.sin(ang)[None, :, None, :]
        x1, x2 = jnp.split(xp.astype(jnp.float32), 2, axis=-1)
        return jnp.concatenate([x1 * cos - x2 * sin, x1 * sin + x2 * cos], axis=-1)

    out = jnp.concatenate([rot(x[..., :half], t // GRID_W), rot(x[..., half:], t % GRID_W)], axis=-1)
    return out.astype(x.dtype)


def context_attention(q, k, v, sink):
    B, S, KV, G, D = q.shape
    nb = S // QBLK
    scale = D ** -0.5
    qb = q.reshape(B, nb, QBLK, KV, G, D).transpose(1, 0, 2, 3, 4, 5)

    def blk(qi):
        s = jnp.einsum('bqkgd,bskd->bkgqs', qi, k, preferred_element_type=jnp.float32) * scale
        if sink is not None:
            s_sink = jnp.broadcast_to(sink.astype(jnp.float32)[None, :, :, None, None], s.shape[:-1] + (1,))
            s = jnp.concatenate([s, s_sink], axis=-1)
        p = jax.nn.softmax(s, axis=-1)[..., :S].astype(v.dtype)
        return jnp.einsum('bkgqs,bskd->bqkgd', p, v)

    o = lax.map(blk, qb)
    return o.transpose(1, 0, 2, 3, 4, 5).reshape(B, S, KV, G, D)


def na_column_tables():
    ncb = GRID_W // NA_COLS
    c = np.arange(GRID_W).reshape(ncb, NA_COLS)
    u = np.clip(c[:, :1] - NA_COLS // 2, 0, GRID_W - 2 * NA_COLS)
    kc = u + np.arange(2 * NA_COLS)[None, :]
    start = np.clip(c - NA_COLS // 2, 0, GRID_W - NA_COLS)
    kcq = kc[:, None, :]
    valid = (kcq >= start[..., None]) & (kcq < start[..., None] + NA_COLS)
    dc = np.clip(kcq - c[..., None] + NA_COLS - 1, 0, 2 * NA_COLS - 2)
    return kc, valid, dc


def neighbourhood_attention(q, k, v, ck, cv, rpb):
    B, N, H, D = q.shape
    rows = N // GRID_W
    wr = min(NA_ROWS, rows)
    ncb = GRID_W // NA_COLS
    L = wr * 2 * NA_COLS
    scale = D ** -0.5
    kc, valid, dc = na_column_tables()
    mask = jnp.asarray(np.broadcast_to(valid[:, :, None, :], (ncb, NA_COLS, wr, 2 * NA_COLS)).reshape(ncb, NA_COLS, L))
    bias_cols = rpb[:, :, dc]
    qg = q.reshape(B, rows, GRID_W, H, D)
    kg = k.reshape(B, rows, GRID_W, H, D)
    vg = v.reshape(B, rows, GRID_W, H, D)

    def row(r):
        rs = jnp.clip(r - wr // 2, 0, rows - wr)
        qr = lax.dynamic_index_in_dim(qg, r, axis=1, keepdims=False).reshape(B, ncb, NA_COLS, H, D)
        kr = lax.dynamic_slice_in_dim(kg, rs, wr, axis=1)[:, :, kc]
        vr = lax.dynamic_slice_in_dim(vg, rs, wr, axis=1)[:, :, kc]
        kr = kr.transpose(0, 2, 1, 3, 4, 5).reshape(B, ncb, L, H, D)
        vr = vr.transpose(0, 2, 1, 3, 4, 5).reshape(B, ncb, L, H, D)
        dr = rs + jnp.arange(wr) - r + NA_ROWS - 1
        bias = jnp.take(bias_cols, dr, axis=1)
        bias = bias.transpose(0, 2, 3, 1, 4).reshape(H, ncb, NA_COLS, L).astype(jnp.float32)
        s_loc = jnp.einsum('bnqhd,bnkhd->bhnqk', qr, kr, preferred_element_type=jnp.float32) * scale + bias
        s_loc = jnp.where(mask, s_loc, NEG)
        s_ctx = jnp.einsum('bnqhd,bshd->bhnqs', qr, ck, preferred_element_type=jnp.float32) * scale
        p = jax.nn.softmax(jnp.concatenate([s_loc, s_ctx], axis=-1), axis=-1).astype(v.dtype)
        o = jnp.einsum('bhnqk,bnkhd->bnqhd', p[..., :L], vr) + jnp.einsum('bhnqs,bshd->bnqhd', p[..., L:], cv)
        return o.reshape(B, GRID_W, H, D)

    o = lax.map(row, jnp.arange(rows))
    return o.transpose(1, 0, 2, 3, 4).reshape(B, N, H, D)


def band_attention(q, k, v, ck, cv, sink):
    B, N, KV, G, D = q.shape
    P = ck.shape[1]
    nb = N // QBLK
    scale = D ** -0.5
    kp = jnp.pad(k, ((0, 0), (QBLK, QBLK), (0, 0), (0, 0)))
    vp = jnp.pad(v, ((0, 0), (QBLK, QBLK), (0, 0), (0, 0)))
    qb = q.reshape(B, nb, QBLK, KV, G, D).transpose(1, 0, 2, 3, 4, 5)
    qpos = jnp.arange(QBLK)[:, None]
    kpos = jnp.arange(3 * QBLK)[None, :] - QBLK

    def blk(args):
        i, qi = args
        kb = lax.dynamic_slice_in_dim(kp, i * QBLK, 3 * QBLK, axis=1)
        vb = lax.dynamic_slice_in_dim(vp, i * QBLK, 3 * QBLK, axis=1)
        kabs = i * QBLK + kpos
        valid = (jnp.abs(kpos - qpos) <= BAND) & (kabs >= 0) & (kabs < N)
        s_loc = jnp.einsum('bqkgd,bskd->bkgqs', qi, kb, preferred_element_type=jnp.float32) * scale
        s_loc = jnp.where(valid, s_loc, NEG)
        s_ctx = jnp.einsum('bqkgd,bskd->bkgqs', qi, ck, preferred_element_type=jnp.float32) * scale
        s_sink = jnp.broadcast_to(sink.astype(jnp.float32)[None, :, :, None, None], s_loc.shape[:-1] + (1,))
        p = jax.nn.softmax(jnp.concatenate([s_loc, s_ctx, s_sink], axis=-1), axis=-1).astype(v.dtype)
        o = jnp.einsum('bkgqs,bskd->bqkgd', p[..., :3 * QBLK], vb)
        return o + jnp.einsum('bkgqs,bskd->bqkgd', p[..., 3 * QBLK:3 * QBLK + P], cv)

    o = lax.map(blk, (jnp.arange(nb), qb))
    return o.transpose(1, 0, 2, 3, 4, 5).reshape(B, N, KV, G, D)


def multiscale_pool(u, w_pool, pool_scale):
    B, T, C = u.shape
    cs = jnp.pad(jnp.cumsum(u.astype(jnp.float32), axis=1), ((0, 0), (1, 0), (0, 0)))
    t = np.arange(T)
    outs = []
    for g, w in enumerate(POOL_WINDOWS):
        lo = np.maximum(t - w // 2, 0)
        hi = np.minimum(t + w // 2, T)
        cnt = (hi - lo).astype(np.float32)[None, :, None]
        csg = cs[:, :, g * POOL_GW:(g + 1) * POOL_GW]
        mean = (csg[:, hi] - csg[:, lo]) / cnt
        outs.append(mean - u[:, :, g * POOL_GW:(g + 1) * POOL_GW].astype(jnp.float32))
    d = jnp.stack(outs, axis=2).astype(u.dtype)
    y = jnp.einsum('btgc,gce->btge', d, w_pool).reshape(B, T, C)
    return y * pool_scale


def conformer_conv(p, conv_w, conv_b, ln_g, ln_b, w_pw2, b_pw2):
    a, b = jnp.split(p, 2, axis=-1)
    h = a * jax.nn.sigmoid(b)
    h = lax.conv_general_dilated(h, conv_w[:, None, :], window_strides=(1,),
                                 padding=[(CONV_K // 2, CONV_K // 2)],
                                 dimension_numbers=('NWC', 'WIO', 'NWC'),
                                 feature_group_count=h.shape[-1]) + conv_b
    h = jax.nn.silu(layernorm(h, ln_g, ln_b))
    return h @ w_pw2 + b_pw2


def mixer_layer(x, cond, ctx_kv, norm_g, w_ada, b_ada, w_in, rpb_a, sink_b, w_pool, pool_scale,
                conv_w, conv_b, ln_g, ln_b, w_pw2, b_pw2, w_out):
    B, T, _ = x.shape
    shift, scale, gate = jnp.split(jax.nn.silu(cond) @ w_ada + b_ada, 3, axis=-1)
    h = rmsnorm(x, norm_g) * (1 + scale) + shift
    proj = h @ w_in
    idx = np.cumsum(IN_SPLITS)[:-1].tolist()
    qa, ka, va, za, qb, kb, vb, zb, uc, zc, pd, zd = jnp.split(proj, idx, axis=-1)
    qa = qa.reshape(B, T, H_A, HEAD_DIM)
    ka = ka.reshape(B, T, H_A, HEAD_DIM)
    va = va.reshape(B, T, H_A, HEAD_DIM)
    qb = qb.reshape(B, T, H_B, HEAD_DIM)
    kb = kb.reshape(B, T, KV_B, HEAD_DIM)
    vb = vb.reshape(B, T, KV_B, HEAD_DIM)
    sink = sink_b.reshape(KV_B, G_B)
    if ctx_kv is None:
        oa = context_attention(qa[:, :, :, None, :], ka, va, None)[:, :, :, 0, :]
        ob = context_attention(qb.reshape(B, T, KV_B, G_B, HEAD_DIM), kb, vb, sink)
        new_kv = (ka, va, kb, vb)
    else:
        cak, cav, cbk, cbv = ctx_kv
        oa = neighbourhood_attention(qa, ka, va, cak, cav, rpb_a)
        qr = axial_rope(qb).reshape(B, T, KV_B, G_B, HEAD_DIM)
        ob = band_attention(qr, axial_rope(kb), vb, cbk, cbv, sink)
        new_kv = None
    oc = multiscale_pool(uc, w_pool, pool_scale)
    od = conformer_conv(pd, conv_w, conv_b, ln_g, ln_b, w_pw2, b_pw2)
    y = jnp.concatenate([oa.reshape(B, T, W_A) * jax.nn.silu(za),
                         ob.reshape(B, T, W_B) * jax.nn.silu(zb),
                         oc * jax.nn.silu(zc),
                         od * jax.nn.silu(zd)], axis=-1)
    return x + gate * (y @ w_out), new_kv


def setup_inputs(seed: int = 0) -> dict:
    key = jax.random.key(seed)
    ks = jax.random.split(key, 26)

    def n(k, s):
        return jax.random.normal(k, s, dtype=jnp.float32)

    return {
        'x_prompt': n(ks[0], (BATCH, SEQ, D_MODEL)),
        'x_sample': n(ks[1], (DEC_BATCH, DEC_SEQ, D_MODEL)),
        'cache_a_k': n(ks[2], (DEC_BATCH, DEPTH, PAST_LEN, H_A, HEAD_DIM)),
        'cache_a_v': n(ks[3], (DEC_BATCH, DEPTH, PAST_LEN, H_A, HEAD_DIM)),
        'cache_b_k': n(ks[4], (DEC_BATCH, DEPTH, PAST_LEN, KV_B, HEAD_DIM)),
        'cache_b_v': n(ks[5], (DEC_BATCH, DEPTH, PAST_LEN, KV_B, HEAD_DIM)),
        'c': n(ks[6], (DEC_BATCH, D_MODEL)),
        'c_ctx': n(ks[7], (D_MODEL,)),
        'norm_g': 1.0 + 0.02 * n(ks[8], (DEPTH, D_MODEL)),
        'w_ada': n(ks[9], (DEPTH, D_MODEL, 3 * D_MODEL)) * (0.5 * D_MODEL ** -0.5),
        'b_ada': 0.01 * n(ks[10], (DEPTH, 3 * D_MODEL)),
        'w_in': n(ks[11], (DEPTH, D_MODEL, IN_W)) * D_MODEL ** -0.5,
        'rpb_a': 0.1 * n(ks[12], (DEPTH, H_A, 2 * NA_ROWS - 1, 2 * NA_COLS - 1)),
        'sink_b': 0.5 * n(ks[13], (DEPTH, H_B)),
        'w_pool': n(ks[14], (DEPTH, N_POOL, POOL_GW, POOL_GW)) * POOL_GW ** -0.5,
        'pool_scale': 1.0 + 0.02 * n(ks[15], (DEPTH, W_C)),
        'conv_w': n(ks[16], (DEPTH, CONV_K, W_D)) * CONV_K ** -0.5,
        'conv_b': 0.01 * n(ks[17], (DEPTH, W_D)),
        'ln_g': 1.0 + 0.02 * n(ks[18], (DEPTH, W_D)),
        'ln_b': 0.01 * n(ks[19], (DEPTH, W_D)),
        'w_pw2': n(ks[20], (DEPTH, W_D, W_D)) * W_D ** -0.5,
        'b_pw2': 0.01 * n(ks[21], (DEPTH, W_D)),
        'w_out': n(ks[22], (DEPTH, MIX_W, D_MODEL)) * MIX_W ** -0.5,
        'final_g': 1.0 + 0.02 * n(ks[23], (D_MODEL,)),
    }


def reference(x_prompt, x_sample, cache_a_k, cache_a_v, cache_b_k, cache_b_v, c, c_ctx,
              norm_g, w_ada, b_ada, w_in, rpb_a, sink_b, w_pool, pool_scale,
              conv_w, conv_b, ln_g, ln_b, w_pw2, b_pw2, w_out, final_g):
    xp = x_prompt
    xs = x_sample
    cond_ctx = c_ctx[None, None, :]
    cond_lat = c[:, None, :]
    ak, av, bk, bv = [], [], [], []
    for l in range(DEPTH):
        lw = (norm_g[l], w_ada[l], b_ada[l], w_in[l], rpb_a[l], sink_b[l], w_pool[l], pool_scale[l],
              conv_w[l], conv_b[l], ln_g[l], ln_b[l], w_pw2[l], b_pw2[l], w_out[l])
        xp, kv = mixer_layer(xp, cond_ctx, None, *lw)
        ak.append(kv[0])
        av.append(kv[1])
        bk.append(kv[2])
        bv.append(kv[3])
        cache_l = (cache_a_k[:, l], cache_a_v[:, l], cache_b_k[:, l], cache_b_v[:, l])
        xs, _ = mixer_layer(xs, cond_lat, cache_l, *lw)
    y_prompt = rmsnorm(xp, final_g)
    y_sample = rmsnorm(xs, final_g)
    new_a_k = jnp.stack(ak, axis=1)
    new_a_v = jnp.stack(av, axis=1)
    new_b_k = jnp.stack(bk, axis=1)
    new_b_v = jnp.stack(bv, axis=1)
    return (y_prompt, y_sample, new_a_k, new_a_v, new_b_k, new_b_v)
```

```python
import functools

import numpy as np
import jax
import jax.numpy as jnp
from jax import lax
from jax.experimental import pallas as pl
from jax.experimental.pallas import tpu as pltpu

D_MODEL = 4096
DEPTH = 2
GRID_W = 64
HEAD_DIM = 128
W_A = D_MODEL // 4
W_B = D_MODEL // 4
W_C = D_MODEL // 4
W_D = D_MODEL - W_A - W_B - W_C
H_A = W_A // HEAD_DIM
H_B = W_B // HEAD_DIM
KV_B = max(1, H_B // 4)
G_B = H_B // KV_B
NA_ROWS = 8
NA_COLS = 16
BAND = 128
N_POOL = 4
POOL_WINDOWS = (2, 4, 8, 16)
POOL_GW = W_C // N_POOL
CONV_K = 31
ROPE_BASE = 10000.0
EPS = 1e-6
NEG = -1e30
IN_W = 4 * W_A + 2 * W_B + 2 * KV_B * HEAD_DIM + 2 * W_C + 3 * W_D

LANE = 128
N_CHUNK = IN_W // LANE
CPG = W_A // LANE
G_QA, G_KA, G_VA, G_ZA, G_QB, G_ZB, G_UC, G_ZC, G_PA, G_PB, G_ZD = range(11)
C_KB = 11 * CPG
C_VB = C_KB + KV_B
HALO = 16
NA_QR = 4
NA_KR = 12
VMEM_LIMIT = 56 * 1024 * 1024

F32 = jnp.float32
BF16 = jnp.bfloat16


def _silu(x):
    return x * jax.nn.sigmoid(x)


def _dot(a, b):
    return jnp.dot(a, b, preferred_element_type=F32)


def _dot_nt(a, b):
    return lax.dot_general(a, b, (((1,), (1,)), ((), ())), preferred_element_type=F32)


def _params(sem):
    return pltpu.CompilerParams(dimension_semantics=sem, vmem_limit_bytes=VMEM_LIMIT)


def _ada_kernel(c_ref, w_ref, b_ref, o_ref):
    s = _silu(c_ref[...])
    o_ref[...] = _dot(s.astype(BF16), w_ref[...].astype(BF16)) + b_ref[...]


def _ada(cond8, w_ada, b_ada):
    tn = 512
    n = 3 * D_MODEL
    return pl.pallas_call(
        _ada_kernel,
        grid=(DEPTH, n // tn),
        in_specs=[
            pl.BlockSpec((8, D_MODEL), lambda l, j: (0, 0)),
            pl.BlockSpec((None, D_MODEL, tn), lambda l, j: (l, 0, j)),
            pl.BlockSpec((None, 1, tn), lambda l, j: (l, 0, j)),
        ],
        out_specs=pl.BlockSpec((None, 8, tn), lambda l, j: (l, 0, j)),
        out_shape=jax.ShapeDtypeStruct((DEPTH, 8, n), F32),
        compiler_params=_params(("arbitrary", "arbitrary")),
        name="ada",
    )(cond8, w_ada, b_ada.reshape(DEPTH, 1, n))


def _inproj_kernel(x_ref, g_ref, sh_ref, sc_ref, w_ref, o_ref, h_ref, *, tm, tn):
    rows = 32

    @pl.when(pl.program_id(1) == 0)
    def _():
        g = g_ref[...]
        sc = 1.0 + sc_ref[...]
        sh = sh_ref[...]

        def body(r, carry):
            r0 = pl.multiple_of(r * rows, rows)
            x = x_ref[pl.ds(r0, rows), :]
            ms = jnp.mean(x * x, axis=-1, keepdims=True)
            y = x * lax.rsqrt(ms + EPS) * g
            h_ref[pl.ds(r0, rows), :] = (y * sc + sh).astype(BF16)
            return carry

        lax.fori_loop(0, tm // rows, body, 0)

    acc = _dot(h_ref[...], w_ref[...])
    for s in range(tn // LANE):
        o_ref[s] = acc[:, s * LANE:(s + 1) * LANE]


def _inproj(x, norm_g, shift, scale, w_bf, layer, tokens_per_cond, tm, tn=512):
    m = x.shape[0]
    tpc = tokens_per_cond // tm
    return pl.pallas_call(
        functools.partial(_inproj_kernel, tm=tm, tn=tn),
        grid=(m // tm, IN_W // tn),
        in_specs=[
            pl.BlockSpec((tm, D_MODEL), lambda i, j: (i, 0)),
            pl.BlockSpec((1, D_MODEL), lambda i, j: (0, 0)),
            pl.BlockSpec((None, 1, D_MODEL), lambda i, j: (i // tpc, 0, 0)),
            pl.BlockSpec((None, 1, D_MODEL), lambda i, j: (i // tpc, 0, 0)),
            pl.BlockSpec((None, D_MODEL, tn), lambda i, j: (layer, 0, j)),
        ],
        out_specs=pl.BlockSpec((tn // LANE, tm, LANE), lambda i, j: (j, i, 0)),
        out_shape=jax.ShapeDtypeStruct((N_CHUNK, m, LANE), F32),
        scratch_shapes=[pltpu.VMEM((tm, D_MODEL), BF16)],
        compiler_params=_params(("arbitrary", "arbitrary")),
        name="inproj",
    )(x, norm_g.reshape(1, D_MODEL), shift, scale, w_bf)


def _ctx_attn_kernel(sink_ref, qa, ka, va, za, qb, zb, kb, vb, ya, yb, nak, nav, nbk, nbv):
    scale = HEAD_DIM ** -0.5
    for h in range(H_A):
        cols = slice(h * HEAD_DIM, (h + 1) * HEAD_DIM)
        k32 = ka[h]
        v32 = va[h]
        s = _dot_nt(qa[h].astype(BF16), k32.astype(BF16)) * scale
        p = jnp.exp(s - jnp.max(s, axis=-1, keepdims=True))
        o = _dot(p.astype(BF16), v32.astype(BF16)) / jnp.sum(p, axis=-1, keepdims=True)
        ya[:, cols] = (o * _silu(za[h])).astype(BF16)
        nak[:, cols] = k32
        nav[:, cols] = v32
    for kv in range(KV_B):
        cols = slice(kv * HEAD_DIM, (kv + 1) * HEAD_DIM)
        nbk[:, cols] = kb[kv]
        nbv[:, cols] = vb[kv]
    for h in range(H_B):
        cols = slice(h * HEAD_DIM, (h + 1) * HEAD_DIM)
        kv = h // G_B
        sink = sink_ref[h]
        s = _dot_nt(qb[h].astype(BF16), kb[kv].astype(BF16)) * scale
        m = jnp.maximum(jnp.max(s, axis=-1, keepdims=True), sink)
        p = jnp.exp(s - m)
        l = jnp.sum(p, axis=-1, keepdims=True) + jnp.exp(sink - m)
        o = _dot(p.astype(BF16), vb[kv].astype(BF16)) / l
        yb[:, cols] = (o * _silu(zb[h])).astype(BF16)


def _ctx_attn(proj, sink, nb, t):
    m = nb * t

    def grp(g):
        return pl.BlockSpec((CPG, t, LANE), lambda b: (g, b, 0))

    def kvs(c):
        return pl.BlockSpec((KV_B, t, LANE), lambda b: (c // KV_B, b, 0))

    wide = pl.BlockSpec((t, W_A), lambda b: (b, 0))
    wide3 = pl.BlockSpec((None, t, W_A), lambda b: (b, 0, 0))
    narrow3 = pl.BlockSpec((None, t, KV_B * HEAD_DIM), lambda b: (b, 0, 0))
    return pl.pallas_call(
        _ctx_attn_kernel,
        grid=(nb,),
        in_specs=[pl.BlockSpec(memory_space=pltpu.SMEM),
                  grp(G_QA), grp(G_KA), grp(G_VA), grp(G_ZA), grp(G_QB), grp(G_ZB), kvs(C_KB), kvs(C_VB)],
        out_specs=[wide, wide, wide3, wide3, narrow3, narrow3],
        out_shape=[jax.ShapeDtypeStruct((m, W_A), BF16), jax.ShapeDtypeStruct((m, W_B), BF16),
                   jax.ShapeDtypeStruct((nb, t, W_A), F32), jax.ShapeDtypeStruct((nb, t, W_A), F32),
                   jax.ShapeDtypeStruct((nb, t, KV_B * HEAD_DIM), F32),
                   jax.ShapeDtypeStruct((nb, t, KV_B * HEAD_DIM), F32)],
        compiler_params=_params(("arbitrary",)),
        name="ctx_attn",
    )(sink, proj, proj, proj, proj, proj, proj, proj, proj)


def _na_tables(rpb):
    rows = GRID_W
    i = np.arange(NA_QR)[:, None, None, None]
    c = np.arange(GRID_W)[None, :, None, None]
    j = np.arange(NA_KR)[None, None, :, None]
    kc = np.arange(GRID_W)[None, None, None, :]
    start = np.clip(c - NA_COLS // 2, 0, GRID_W - NA_COLS)
    col_ok = (kc >= start) & (kc < start + NA_COLS)
    dc = np.clip(kc - c + NA_COLS - 1, 0, 2 * NA_COLS - 2)
    masks, drs = [], []
    last_r0 = rows - NA_QR
    last_ks = rows - NA_KR
    for r0, ks in ((0, 0), (NA_QR, 0), (last_r0, last_ks)):
        r = r0 + i
        rs = np.clip(r - NA_ROWS // 2, 0, rows - NA_ROWS)
        kr = ks + j
        row_ok = (kr >= rs) & (kr < rs + NA_ROWS)
        drs.append(np.clip(kr - r + NA_ROWS - 1, 0, 2 * NA_ROWS - 2))
        masks.append(row_ok & col_ok)
    shape = (NA_QR, GRID_W, NA_KR, GRID_W)
    flat = (NA_QR * GRID_W, NA_KR * GRID_W)
    mask = np.stack([np.broadcast_to(mk, shape).reshape(flat) for mk in masks]).astype(np.float32)
    dr = np.stack([np.broadcast_to(d, shape).reshape(flat) for d in drs])
    dcb = np.broadcast_to(dc, shape).reshape(flat)
    bias = rpb[:, dr, dcb[None]]
    return bias.astype(F32), jnp.asarray(mask)


def _na_kernel(q_ref, k_ref, v_ref, z_ref, ck_ref, cv_ref, bias_ref, mask_ref, o_ref, *, n_blocks):
    scale = HEAD_DIM ** -0.5
    qn = NA_QR * GRID_W
    kn = NA_KR * GRID_W
    ck = ck_ref[...].astype(BF16)
    cv = cv_ref[...].astype(BF16)

    def body(rb, carry):
        q0 = pl.multiple_of(rb * qn, qn)
        ks = jnp.clip(rb * NA_QR - NA_ROWS // 2, 0, n_blocks * NA_QR - NA_KR)
        k0 = pl.multiple_of(ks * GRID_W, GRID_W)
        ty = jnp.where(rb == 0, 0, jnp.where(rb == n_blocks - 1, 2, 1))
        q = q_ref[pl.ds(q0, qn), :].astype(BF16)
        k = k_ref[pl.ds(k0, kn), :].astype(BF16)
        v = v_ref[pl.ds(k0, kn), :].astype(BF16)
        s_loc = _dot_nt(q, k) * scale + bias_ref[ty]
        s_loc = jnp.where(mask_ref[ty] > 0.5, s_loc, NEG)
        s_ctx = _dot_nt(q, ck) * scale
        m = jnp.maximum(jnp.max(s_loc, axis=-1, keepdims=True), jnp.max(s_ctx, axis=-1, keepdims=True))
        p_loc = jnp.exp(s_loc - m)
        p_ctx = jnp.exp(s_ctx - m)
        l = jnp.sum(p_loc, axis=-1, keepdims=True) + jnp.sum(p_ctx, axis=-1, keepdims=True)
        o = (_dot(p_loc.astype(BF16), v) + _dot(p_ctx.astype(BF16), cv)) / l
        o_ref[pl.ds(q0, qn), :] = (o * _silu(z_ref[pl.ds(q0, qn), :])).astype(BF16)
        return carry

    lax.fori_loop(0, n_blocks, body, 0)


def _na_attn(proj, cache_k, cache_v, bias, mask, layer, nb, t):
    p_len = cache_k.shape[2]
    n_blocks = t // (NA_QR * GRID_W)

    def head(g):
        return pl.BlockSpec((None, t, LANE), lambda b, h: (g * CPG + h, b, 0))

    ctx = pl.BlockSpec((None, None, p_len, HEAD_DIM), lambda b, h: (b, layer, 0, h))
    return pl.pallas_call(
        functools.partial(_na_kernel, n_blocks=n_blocks),
        grid=(nb, H_A),
        in_specs=[head(G_QA), head(G_KA), head(G_VA), head(G_ZA), ctx, ctx,
                  pl.BlockSpec((None,) + bias.shape[1:], lambda b, h: (h, 0, 0, 0)),
                  pl.BlockSpec(mask.shape, lambda b, h: (0, 0, 0))],
        out_specs=pl.BlockSpec((t, HEAD_DIM), lambda b, h: (b, h)),
        out_shape=jax.ShapeDtypeStruct((nb * t, W_A), BF16),
        compiler_params=_params(("arbitrary", "arbitrary")),
        name="na_attn",
    )(proj, proj, proj, proj, cache_k, cache_v, bias, mask)


def _rope_tables(t):
    pos = np.arange(t)
    half = HEAD_DIM // 2
    inv = 1.0 / (ROPE_BASE ** (np.arange(0, half, 2, dtype=np.float64) / half))
    ang_r = (pos // GRID_W)[:, None] * inv[None, :]
    ang_c = (pos % GRID_W)[:, None] * inv[None, :]
    zero = np.zeros_like(ang_r)
    cos = np.concatenate([np.cos(ang_r), np.cos(ang_r), np.cos(ang_c), np.cos(ang_c)], axis=1)
    s_up = np.concatenate([-np.sin(ang_r), zero, -np.sin(ang_c), zero], axis=1)
    s_dn = np.concatenate([zero, np.sin(ang_r), zero, np.sin(ang_c)], axis=1)
    return tuple(jnp.asarray(a.astype(np.float32)) for a in (cos, s_up, s_dn))


def _band_mask(t):
    row = np.tile(np.arange(BAND), G_B)[:, None]
    col = np.arange(3 * BAND)[None, :]
    out = []
    for off in (0, -BAND, -2 * BAND):
        out.append(np.abs(off + col - row) <= BAND)
    return jnp.asarray(np.stack(out).astype(np.float32))


def _rope(x, cos, s_up, s_dn):
    quarter = HEAD_DIM // 4
    return x * cos + pltpu.roll(x, HEAD_DIM - quarter, 1) * s_up + pltpu.roll(x, quarter, 1) * s_dn


def _band_kernel(sink_ref, q_ref, k_ref, v_ref, z_ref, ck_ref, cv_ref, cos_ref, up_ref, dn_ref, mask_ref,
                 o_ref, kr_ref, vr_ref, *, n_blocks, blocks_per_step):
    scale = HEAD_DIM ** -0.5
    kv = pl.program_id(1)
    step = pl.program_id(2)
    t = n_blocks * BAND
    chunk = 512

    @pl.when(step == 0)
    def _():
        def prep(i, carry):
            r0 = pl.multiple_of(i * chunk, chunk)
            sl = pl.ds(r0, chunk)
            kr_ref[sl, :] = _rope(k_ref[sl, :], cos_ref[sl, :], up_ref[sl, :], dn_ref[sl, :]).astype(BF16)
            vr_ref[sl, :] = v_ref[sl, :].astype(BF16)
            return carry

        lax.fori_loop(0, t // chunk, prep, 0)

    ck = ck_ref[...].astype(BF16)
    cv = cv_ref[...].astype(BF16)
    sink = jnp.concatenate([jnp.full((BAND, 1), sink_ref[kv * G_B + g], F32) for g in range(G_B)], axis=0)

    def body(j, carry):
        i = step * blocks_per_step + j
        ql = pl.ds(pl.multiple_of(j * BAND, BAND), BAND)
        qs = pl.ds(pl.multiple_of(i * BAND, BAND), BAND)
        kb = jnp.clip(i - 1, 0, n_blocks - 3)
        k0 = pl.multiple_of(kb * BAND, BAND)
        ty = jnp.where(i == 0, 0, jnp.where(i == n_blocks - 1, 2, 1))
        cos = cos_ref[qs, :]
        up = up_ref[qs, :]
        dn = dn_ref[qs, :]
        q = jnp.concatenate([_rope(q_ref[g, ql, :], cos, up, dn).astype(BF16) for g in range(G_B)], axis=0)
        k = kr_ref[pl.ds(k0, 3 * BAND), :]
        v = vr_ref[pl.ds(k0, 3 * BAND), :]
        s_loc = jnp.where(mask_ref[ty] > 0.5, _dot_nt(q, k) * scale, NEG)
        s_ctx = _dot_nt(q, ck) * scale
        m = jnp.maximum(jnp.maximum(jnp.max(s_loc, axis=-1, keepdims=True),
                                    jnp.max(s_ctx, axis=-1, keepdims=True)), sink)
        p_loc = jnp.exp(s_loc - m)
        p_ctx = jnp.exp(s_ctx - m)
        l = jnp.sum(p_loc, axis=-1, keepdims=True) + jnp.sum(p_ctx, axis=-1, keepdims=True) + jnp.exp(sink - m)
        o = (_dot(p_loc.astype(BF16), v) + _dot(p_ctx.astype(BF16), cv)) / l
        for g in range(G_B):
            og = o[g * BAND:(g + 1) * BAND, :] * _silu(z_ref[g, ql, :])
            o_ref[ql, g * HEAD_DIM:(g + 1) * HEAD_DIM] = og.astype(BF16)
        return carry

    lax.fori_loop(0, blocks_per_step, body, 0)


def _band_attn(proj, sink, cache_k, cache_v, rope, mask, layer, nb, t, tq=1024):
    p_len = cache_k.shape[2]
    n_blocks = t // BAND
    nq = t // tq
    cos, s_up, s_dn = rope
    heads = pl.BlockSpec((G_B, tq, LANE), lambda b, kv, s: (G_QB * CPG // G_B + kv, b * nq + s, 0))
    gates = pl.BlockSpec((G_B, tq, LANE), lambda b, kv, s: (G_ZB * CPG // G_B + kv, b * nq + s, 0))
    keys = pl.BlockSpec((None, t, LANE), lambda b, kv, s: (C_KB + kv, b, 0))
    vals = pl.BlockSpec((None, t, LANE), lambda b, kv, s: (C_VB + kv, b, 0))
    ctx = pl.BlockSpec((None, None, p_len, HEAD_DIM), lambda b, kv, s: (b, layer, 0, kv))
    tab = pl.BlockSpec((t, HEAD_DIM), lambda b, kv, s: (0, 0))
    return pl.pallas_call(
        functools.partial(_band_kernel, n_blocks=n_blocks, blocks_per_step=tq // BAND),
        grid=(nb, KV_B, nq),
        in_specs=[pl.BlockSpec(memory_space=pltpu.SMEM), heads, keys, vals, gates, ctx, ctx, tab, tab, tab,
                  pl.BlockSpec(mask.shape, lambda b, kv, s: (0, 0, 0))],
        out_specs=pl.BlockSpec((tq, G_B * HEAD_DIM), lambda b, kv, s: (b * nq + s, kv)),
        out_shape=jax.ShapeDtypeStruct((nb * t, W_B), BF16),
        scratch_shapes=[pltpu.VMEM((t, HEAD_DIM), BF16), pltpu.VMEM((t, HEAD_DIM), BF16)],
        compiler_params=_params(("arbitrary", "arbitrary", "arbitrary")),
        name="band_attn",
    )(sink, proj, proj, proj, proj, cache_k, cache_v, cos, s_up, s_dn, mask)


def _cd_kernel(uc, ucp, ucn, zc, pa, pap, pan, pb, pbp, pbn, zd, wpool, pscale, cw, cb, lng, lnb, wpw, bpw,
               o_ref, ue, he, cv, hn, *, tt, t_seq):
    i = pl.program_id(1)
    has_prev = i > 0
    has_next = i < pl.num_programs(1) - 1
    ext = tt + 2 * HALO

    for c in range(CPG):
        ue[c, 0:HALO, :] = jnp.where(has_prev, ucp[c], 0.0)
        ue[c, HALO:HALO + tt, :] = uc[c]
        ue[c, HALO + tt:ext, :] = jnp.where(has_next, ucn[c], 0.0)
        he[c, 0:HALO, :] = jnp.where(has_prev, pap[c] * jax.nn.sigmoid(pbp[c]), 0.0)
        he[c, HALO:HALO + tt, :] = pa[c] * jax.nn.sigmoid(pb[c])
        he[c, HALO + tt:ext, :] = jnp.where(has_next, pan[c] * jax.nn.sigmoid(pbn[c]), 0.0)

    rt = 64
    for r0 in range(0, tt, rt):
        tpos = i * tt + r0 + lax.broadcasted_iota(jnp.int32, (rt, LANE), 0)
        for g, w in enumerate(POOL_WINDOWS):
            half = w // 2
            cnt = (jnp.minimum(tpos + half, t_seq) - jnp.maximum(tpos - half, 0)).astype(F32)
            d = []
            for c in (2 * g, 2 * g + 1):
                acc = ue[c, HALO + r0 - half:HALO + r0 - half + rt, :]
                for o in range(-half + 1, half):
                    acc = acc + ue[c, HALO + r0 + o:HALO + r0 + o + rt, :]
                d.append(acc / cnt - uc[c, r0:r0 + rt, :])
            dg = jnp.concatenate(d, axis=1).astype(BF16)
            cols = slice(g * POOL_GW, (g + 1) * POOL_GW)
            y = _dot(dg, wpool[g]) * pscale[:, cols]
            z = jnp.concatenate([zc[2 * g, r0:r0 + rt, :], zc[2 * g + 1, r0:r0 + rt, :]], axis=1)
            o_ref[r0:r0 + rt, cols] = (y * _silu(z)).astype(BF16)

    def conv_chunk(c, carry):
        wts = cw[c]
        bias = cb[c]
        for r0 in range(0, tt, rt):
            acc = jnp.zeros((rt, LANE), F32) + bias
            for k in range(CONV_K):
                off = HALO - CONV_K // 2 + k + r0
                acc = acc + he[c, off:off + rt, :] * wts[k:k + 1, :]
            cv[c, r0:r0 + rt, :] = acc
        return carry

    lax.fori_loop(0, CPG, conv_chunk, 0)

    def ln_rows(r, carry):
        r0 = pl.multiple_of(r * rt, rt)
        rs = pl.ds(r0, rt)
        xs = [cv[c, rs, :] for c in range(CPG)]
        tot = xs[0]
        for x in xs[1:]:
            tot = tot + x
        mu = jnp.sum(tot, axis=-1, keepdims=True) * (1.0 / W_D)
        sq = jnp.square(xs[0] - mu)
        for x in xs[1:]:
            sq = sq + jnp.square(x - mu)
        rstd = lax.rsqrt(jnp.sum(sq, axis=-1, keepdims=True) * (1.0 / W_D) + EPS)
        for c in range(CPG):
            cols = slice(c * LANE, (c + 1) * LANE)
            y = (xs[c] - mu) * rstd * lng[:, cols] + lnb[:, cols]
            hn[rs, cols] = _silu(y).astype(BF16)
        return carry

    lax.fori_loop(0, tt // rt, ln_rows, 0)

    for r0 in range(0, tt, 256):
        rs = slice(r0, r0 + 256)
        y = _dot(hn[rs, :], wpw[...]) + bpw[...]
        for c in range(CPG):
            cols = slice(c * LANE, (c + 1) * LANE)
            o_ref[rs, W_C + c * LANE:W_C + (c + 1) * LANE] = (y[:, cols] * _silu(zd[c, rs, :])).astype(BF16)


def _cd_mixers(proj, w_pool_bf, pool_scale, conv_w, conv_b, ln_g, ln_b, w_pw2_bf, b_pw2, nb, t, tt):
    m = nb * t
    nt = t // tt
    hb = tt // HALO
    last = m // HALO - 1

    def cur(g):
        return pl.BlockSpec((CPG, tt, LANE), lambda b, i: (g, b * nt + i, 0))

    def prev(g):
        return pl.BlockSpec((CPG, HALO, LANE), lambda b, i: (g, jnp.maximum((b * nt + i) * hb - 1, 0), 0))

    def nxt(g):
        return pl.BlockSpec((CPG, HALO, LANE), lambda b, i: (g, jnp.minimum((b * nt + i + 1) * hb, last), 0))

    def full(shape):
        return pl.BlockSpec(shape, lambda b, i: (0,) * len(shape))

    ext = tt + 2 * HALO
    cw = conv_w.reshape(CONV_K, CPG, LANE).transpose(1, 0, 2)
    cb = conv_b.reshape(CPG, 1, LANE)
    return pl.pallas_call(
        functools.partial(_cd_kernel, tt=tt, t_seq=t),
        grid=(nb, nt),
        in_specs=[cur(G_UC), prev(G_UC), nxt(G_UC), cur(G_ZC),
                  cur(G_PA), prev(G_PA), nxt(G_PA), cur(G_PB), prev(G_PB), nxt(G_PB), cur(G_ZD),
                  full((N_POOL, POOL_GW, POOL_GW)), full((1, W_C)), full((CPG, CONV_K, LANE)),
                  full((CPG, 1, LANE)), full((1, W_D)), full((1, W_D)), full((W_D, W_D)), full((1, W_D))],
        out_specs=pl.BlockSpec((tt, W_C + W_D), lambda b, i: (b * nt + i, 0)),
        out_shape=jax.ShapeDtypeStruct((m, W_C + W_D), BF16),
        scratch_shapes=[pltpu.VMEM((CPG, ext, LANE), F32), pltpu.VMEM((CPG, ext, LANE), F32),
                        pltpu.VMEM((CPG, tt, LANE), F32), pltpu.VMEM((tt, W_D), BF16)],
        compiler_params=_params(("arbitrary", "arbitrary")),
        name="cd_mixers",
    )(proj, proj, proj, proj, proj, proj, proj, proj, proj, proj, proj,
      w_pool_bf, pool_scale.reshape(1, W_C), cw, cb, ln_g.reshape(1, W_D), ln_b.reshape(1, W_D),
      w_pw2_bf, b_pw2.reshape(1, W_D))


def _outproj_kernel(ya, yb, ycd, w_ref, x_ref, g_ref, o_ref):
    acc = _dot(ya[...], w_ref[0:W_A, :])
    acc = acc + _dot(yb[...], w_ref[W_A:W_A + W_B, :])
    acc = acc + _dot(ycd[...], w_ref[W_A + W_B:, :])
    o_ref[...] = x_ref[...] + g_ref[...] * acc


def _outproj(ya, yb, ycd, w_bf, x, gate, layer, tokens_per_cond, tm, tn=512):
    m = x.shape[0]
    tpc = tokens_per_cond // tm
    return pl.pallas_call(
        _outproj_kernel,
        grid=(m // tm, D_MODEL // tn),
        in_specs=[
            pl.BlockSpec((tm, W_A), lambda i, j: (i, 0)),
            pl.BlockSpec((tm, W_B), lambda i, j: (i, 0)),
            pl.BlockSpec((tm, W_C + W_D), lambda i, j: (i, 0)),
            pl.BlockSpec((None, D_MODEL, tn), lambda i, j: (layer, 0, j)),
            pl.BlockSpec((tm, tn), lambda i, j: (i, j)),
            pl.BlockSpec((None, 1, tn), lambda i, j: (i // tpc, 0, j)),
        ],
        out_specs=pl.BlockSpec((tm, tn), lambda i, j: (i, j)),
        out_shape=jax.ShapeDtypeStruct((m, D_MODEL), F32),
        compiler_params=_params(("arbitrary", "arbitrary")),
        name="outproj",
    )(ya, yb, ycd, w_bf, x, gate)


def _rmsnorm_kernel(x_ref, g_ref, o_ref):
    x = x_ref[...]
    ms = jnp.mean(x * x, axis=-1, keepdims=True)
    o_ref[...] = x * lax.rsqrt(ms + EPS) * g_ref[...]


def _final_norm(x, g, tm=256):
    m = x.shape[0]
    return pl.pallas_call(
        _rmsnorm_kernel,
        grid=(m // tm,),
        in_specs=[pl.BlockSpec((tm, D_MODEL), lambda i: (i, 0)), pl.BlockSpec((1, D_MODEL), lambda i: (0, 0))],
        out_specs=pl.BlockSpec((tm, D_MODEL), lambda i: (i, 0)),
        out_shape=jax.ShapeDtypeStruct((m, D_MODEL), F32),
        compiler_params=_params(("arbitrary",)),
        name="final_norm",
    )(x, g.reshape(1, D_MODEL))


def _permute_in_cols(w_in):
    k0 = 4 * W_A + W_B
    k1 = k0 + 2 * KV_B * HEAD_DIM
    return jnp.concatenate([w_in[..., :k0], w_in[..., k1:], w_in[..., k0:k1]], axis=-1)


def kernel(x_prompt, x_sample, cache_a_k, cache_a_v, cache_b_k, cache_b_v, c, c_ctx, norm_g, w_ada, b_ada, w_in,
           rpb_a, sink_b, w_pool, pool_scale, conv_w, conv_b, ln_g, ln_b, w_pw2, b_pw2, w_out, final_g):
    nbp, tp, _ = x_prompt.shape
    nbs, ts, _ = x_sample.shape
    p_len = cache_a_k.shape[2]
    assert ts == GRID_W * GRID_W and nbs + 1 <= 8

    w_in_bf = _permute_in_cols(w_in).astype(BF16)
    w_out_bf = w_out.astype(BF16)
    w_pool_bf = w_pool.astype(BF16)
    w_pw2_bf = w_pw2.astype(BF16)
    cak = cache_a_k.reshape(nbs, DEPTH, p_len, W_A)
    cav = cache_a_v.reshape(nbs, DEPTH, p_len, W_A)
    cbk = cache_b_k.reshape(nbs, DEPTH, p_len, KV_B * HEAD_DIM)
    cbv = cache_b_v.reshape(nbs, DEPTH, p_len, KV_B * HEAD_DIM)

    cond8 = jnp.concatenate([c_ctx[None, :], c, jnp.zeros((8 - 1 - nbs, D_MODEL), F32)], axis=0)
    ada = _ada(cond8, w_ada, b_ada)
    rope = _rope_tables(ts)
    band_mask = _band_mask(ts)

    xp = x_prompt.reshape(nbp * tp, D_MODEL)
    xs = x_sample.reshape(nbs * ts, D_MODEL)
    kv_out = [[], [], [], []]
    for l in range(DEPTH):
        def mod(k, lo, hi):
            return ada[l, lo:hi, k * D_MODEL:(k + 1) * D_MODEL].reshape(hi - lo, 1, D_MODEL)

        na_bias, na_mask = _na_tables(rpb_a[l])

        proj = _inproj(xp, norm_g[l], mod(0, 0, 1), mod(1, 0, 1), w_in_bf, l, nbp * tp, tm=512)
        ya, yb, nak, nav, nbk, nbv = _ctx_attn(proj, sink_b[l], nbp, tp)
        ycd = _cd_mixers(proj, w_pool_bf[l], pool_scale[l], conv_w[l], conv_b[l], ln_g[l], ln_b[l],
                         w_pw2_bf[l], b_pw2[l], nbp, tp, tt=tp)
        xp = _outproj(ya, yb, ycd, w_out_bf, xp, mod(2, 0, 1), l, nbp * tp, tm=1024)
        for lst, a in zip(kv_out, (nak, nav, nbk, nbv)):
            lst.append(a)

        proj = _inproj(xs, norm_g[l], mod(0, 1, 1 + nbs), mod(1, 1, 1 + nbs), w_in_bf, l, ts, tm=512)
        ya = _na_attn(proj, cak, cav, na_bias, na_mask, l, nbs, ts)
        yb = _band_attn(proj, sink_b[l], cbk, cbv, rope, band_mask, l, nbs, ts)
        ycd = _cd_mixers(proj, w_pool_bf[l], pool_scale[l], conv_w[l], conv_b[l], ln_g[l], ln_b[l],
                         w_pw2_bf[l], b_pw2[l], nbs, ts, tt=512)
        xs = _outproj(ya, yb, ycd, w_out_bf, xs, mod(2, 1, 1 + nbs), l, ts, tm=1024)

    y_prompt = _final_norm(xp, final_g).reshape(nbp, tp, D_MODEL)
    y_sample = _final_norm(xs, final_g).reshape(nbs, ts, D_MODEL)
    new_a_k = jnp.stack(kv_out[0], axis=1).reshape(nbp, DEPTH, tp, H_A, HEAD_DIM)
    new_a_v = jnp.stack(kv_out[1], axis=1).reshape(nbp, DEPTH, tp, H_A, HEAD_DIM)
    new_b_k = jnp.stack(kv_out[2], axis=1).reshape(nbp, DEPTH, tp, KV_B, HEAD_DIM)
    new_b_v = jnp.stack(kv_out[3], axis=1).reshape(nbp, DEPTH, tp, KV_B, HEAD_DIM)
    return (y_prompt, y_sample, new_a_k, new_a_v, new_b_k, new_b_v)
```

```python
import functools

import numpy as np
import jax
import jax.numpy as jnp
from jax import lax
from jax.experimental import pallas as pl
from jax.experimental.pallas import tpu as pltpu

D_MODEL = 4096
DEPTH = 2
GRID_W = 64
HEAD_DIM = 128
W_A = D_MODEL // 4
W_B = D_MODEL // 4
W_C = D_MODEL // 4
W_D = D_MODEL - W_A - W_B - W_C
H_A = W_A // HEAD_DIM
H_B = W_B // HEAD_DIM
KV_B = max(1, H_B // 4)
G_B = H_B // KV_B
NA_ROWS = 8
NA_COLS = 16
BAND = 128
N_POOL = 4
POOL_WINDOWS = (2, 4, 8, 16)
POOL_GW = W_C // N_POOL
CONV_K = 31
ROPE_BASE = 10000.0
EPS = 1e-6
NEG = -1e30
IN_W = 4 * W_A + 2 * W_B + 2 * KV_B * HEAD_DIM + 2 * W_C + 3 * W_D

LANE = 128
N_CHUNK = IN_W // LANE
CPG = W_A // LANE
G_QA, G_KA, G_VA, G_ZA, G_QB, G_ZB, G_UC, G_ZC, G_PA, G_PB, G_ZD = range(11)
C_KB = 11 * CPG
C_VB = C_KB + KV_B
HALO = 16
NA_QR = 4
NA_KR = 12
VMEM_LIMIT = 56 * 1024 * 1024

F32 = jnp.float32
BF16 = jnp.bfloat16


def _silu(x):
    return x * jax.nn.sigmoid(x)


def _dot(a, b):
    return jnp.dot(a, b, preferred_element_type=F32)


def _dot_nt(a, b):
    return lax.dot_general(a, b, (((1,), (1,)), ((), ())), preferred_element_type=F32)


def _params(sem):
    return pltpu.CompilerParams(dimension_semantics=sem, vmem_limit_bytes=VMEM_LIMIT)


def _ada_kernel(c_ref, w_ref, b_ref, o_ref):
    s = _silu(c_ref[...])
    o_ref[...] = _dot(s.astype(BF16), w_ref[...].astype(BF16)) + b_ref[...]


def _ada(cond8, w_ada, b_ada):
    tn = 512
    n = 3 * D_MODEL
    return pl.pallas_call(
        _ada_kernel,
        grid=(DEPTH, n // tn),
        in_specs=[
            pl.BlockSpec((8, D_MODEL), lambda l, j: (0, 0)),
            pl.BlockSpec((None, D_MODEL, tn), lambda l, j: (l, 0, j)),
            pl.BlockSpec((None, 1, tn), lambda l, j: (l, 0, j)),
        ],
        out_specs=pl.BlockSpec((None, 8, tn), lambda l, j: (l, 0, j)),
        out_shape=jax.ShapeDtypeStruct((DEPTH, 8, n), F32),
        compiler_params=_params(("arbitrary", "arbitrary")),
        name="ada",
    )(cond8, w_ada, b_ada.reshape(DEPTH, 1, n))


NORM_ROWS = 32


def _norm_rows(src_ref, n_rows, emit):
    def body(r, carry):
        rs = pl.ds(pl.multiple_of(r * NORM_ROWS, NORM_ROWS), NORM_ROWS)
        x = src_ref[rs, :]
        ms = jnp.mean(x * x, axis=-1, keepdims=True)
        emit(rs, x * lax.rsqrt(ms + EPS))
        return carry

    lax.fori_loop(0, n_rows // NORM_ROWS, body, 0)


def _prenorm_kernel(x_ref, g_ref, sh_ref, sc_ref, h_ref, *, tm):
    g = g_ref[...]
    sc = 1.0 + sc_ref[...]
    sh = sh_ref[...]

    def emit(rs, y):
        h_ref[rs, :] = (y * g * sc + sh).astype(BF16)

    _norm_rows(x_ref, tm, emit)


def _prenorm(x, norm_g, shift, scale, tokens_per_cond, tm=256):
    m = x.shape[0]
    tpc = tokens_per_cond // tm
    cond = pl.BlockSpec((None, 1, D_MODEL), lambda i: (i // tpc, 0, 0))
    return pl.pallas_call(
        functools.partial(_prenorm_kernel, tm=tm),
        grid=(m // tm,),
        in_specs=[pl.BlockSpec((tm, D_MODEL), lambda i: (i, 0)), pl.BlockSpec((1, D_MODEL), lambda i: (0, 0)),
                  cond, cond],
        out_specs=pl.BlockSpec((tm, D_MODEL), lambda i: (i, 0)),
        out_shape=jax.ShapeDtypeStruct((m, D_MODEL), BF16),
        compiler_params=_params(("arbitrary",)),
        name="prenorm",
    )(x, norm_g.reshape(1, D_MODEL), shift, scale)


def _inproj_kernel(h_ref, w_ref, o_ref, *, tn):
    acc = _dot(h_ref[...], w_ref[...])
    for s in range(tn // LANE):
        o_ref[s] = acc[:, s * LANE:(s + 1) * LANE]


def _inproj(h, w_bf, layer, tm, tn=512):
    m = h.shape[0]
    return pl.pallas_call(
        functools.partial(_inproj_kernel, tn=tn),
        grid=(m // tm, IN_W // tn),
        in_specs=[
            pl.BlockSpec((tm, D_MODEL), lambda i, j: (i, 0)),
            pl.BlockSpec((None, D_MODEL, tn), lambda i, j: (layer, 0, j)),
        ],
        out_specs=pl.BlockSpec((tn // LANE, tm, LANE), lambda i, j: (j, i, 0)),
        out_shape=jax.ShapeDtypeStruct((N_CHUNK, m, LANE), F32),
        compiler_params=_params(("arbitrary", "arbitrary")),
        name="inproj",
    )(h, w_bf)


def _ctx_attn_kernel(sink_ref, qa, ka, va, za, qb, zb, kb, vb, ya, yb, nak, nav, nbk, nbv):
    scale = HEAD_DIM ** -0.5
    for h in range(H_A):
        cols = slice(h * HEAD_DIM, (h + 1) * HEAD_DIM)
        k32 = ka[h]
        v32 = va[h]
        s = _dot_nt(qa[h].astype(BF16), k32.astype(BF16)) * scale
        p = jnp.exp(s - jnp.max(s, axis=-1, keepdims=True))
        o = _dot(p.astype(BF16), v32.astype(BF16)) / jnp.sum(p, axis=-1, keepdims=True)
        ya[:, cols] = (o * _silu(za[h])).astype(BF16)
        nak[:, cols] = k32
        nav[:, cols] = v32
    for kv in range(KV_B):
        cols = slice(kv * HEAD_DIM, (kv + 1) * HEAD_DIM)
        nbk[:, cols] = kb[kv]
        nbv[:, cols] = vb[kv]
    for h in range(H_B):
        cols = slice(h * HEAD_DIM, (h + 1) * HEAD_DIM)
        kv = h // G_B
        sink = sink_ref[h]
        s = _dot_nt(qb[h].astype(BF16), kb[kv].astype(BF16)) * scale
        m = jnp.maximum(jnp.max(s, axis=-1, keepdims=True), sink)
        p = jnp.exp(s - m)
        l = jnp.sum(p, axis=-1, keepdims=True) + jnp.exp(sink - m)
        o = _dot(p.astype(BF16), vb[kv].astype(BF16)) / l
        yb[:, cols] = (o * _silu(zb[h])).astype(BF16)


def _ctx_attn(proj, sink, nb, t):
    m = nb * t

    def grp(g):
        return pl.BlockSpec((CPG, t, LANE), lambda b: (g, b, 0))

    def kvs(c):
        return pl.BlockSpec((KV_B, t, LANE), lambda b: (c // KV_B, b, 0))

    wide = pl.BlockSpec((t, W_A), lambda b: (b, 0))
    wide3 = pl.BlockSpec((None, t, W_A), lambda b: (b, 0, 0))
    narrow3 = pl.BlockSpec((None, t, KV_B * HEAD_DIM), lambda b: (b, 0, 0))
    return pl.pallas_call(
        _ctx_attn_kernel,
        grid=(nb,),
        in_specs=[pl.BlockSpec(memory_space=pltpu.SMEM),
                  grp(G_QA), grp(G_KA), grp(G_VA), grp(G_ZA), grp(G_QB), grp(G_ZB), kvs(C_KB), kvs(C_VB)],
        out_specs=[wide, wide, wide3, wide3, narrow3, narrow3],
        out_shape=[jax.ShapeDtypeStruct((m, W_A), BF16), jax.ShapeDtypeStruct((m, W_B), BF16),
                   jax.ShapeDtypeStruct((nb, t, W_A), F32), jax.ShapeDtypeStruct((nb, t, W_A), F32),
                   jax.ShapeDtypeStruct((nb, t, KV_B * HEAD_DIM), F32),
                   jax.ShapeDtypeStruct((nb, t, KV_B * HEAD_DIM), F32)],
        compiler_params=_params(("arbitrary",)),
        name="ctx_attn",
    )(sink, proj, proj, proj, proj, proj, proj, proj, proj)


def _na_tables(rpb):
    rows = GRID_W
    n_dr = 2 * NA_ROWS - 1
    n_dc = 2 * NA_COLS - 1
    c = np.arange(GRID_W)[:, None]
    kc = np.arange(GRID_W)[None, :]
    start = np.clip(c - NA_COLS // 2, 0, GRID_W - NA_COLS)
    col_ok = (kc >= start) & (kc < start + NA_COLS)
    dc = np.clip(kc - c + NA_COLS - 1, 0, n_dc - 1)
    onehot = (dc.reshape(1, -1) == np.arange(n_dc)[:, None]).astype(np.float32)
    toep = jnp.einsum("hrd,dx->hrx", rpb, jnp.asarray(onehot), precision=lax.Precision.HIGHEST)
    toep = toep.reshape(H_A, n_dr, GRID_W, GRID_W)
    padded = jnp.concatenate([toep[:, :1]] * NA_KR + [toep] + [toep[:, -1:]] * NA_KR, axis=1)
    i = np.arange(NA_QR)[:, None]
    j = np.arange(NA_KR)[None, :]
    masks, slabs = [], []
    for r0, ks in ((0, 0), (NA_QR, 0), (rows - NA_QR, rows - NA_KR)):
        r = r0 + i
        rs = np.clip(r - NA_ROWS // 2, 0, rows - NA_ROWS)
        kr = ks + j
        row_ok = (kr >= rs) & (kr < rs + NA_ROWS)
        masks.append(row_ok[:, None, :, None] & col_ok[None, :, None, :])
        for qi in range(NA_QR):
            dr0 = ks - (r0 + qi) + NA_ROWS - 1 + NA_KR
            slabs.append(padded[:, dr0:dr0 + NA_KR])
    flat = (NA_QR * GRID_W, NA_KR * GRID_W)
    mask = np.stack([mk.reshape(flat) for mk in masks]).astype(np.float32)
    bias = jnp.stack(slabs, axis=1).reshape(H_A, 3, NA_QR, NA_KR, GRID_W, GRID_W)
    bias = bias.transpose(0, 1, 2, 4, 3, 5).reshape((H_A, 3) + flat)
    return bias, jnp.asarray(mask)


def _na_kernel(q_ref, k_ref, v_ref, z_ref, ck_ref, cv_ref, bias_ref, mask_ref, o_ref, *, n_blocks):
    scale = HEAD_DIM ** -0.5
    qn = NA_QR * GRID_W
    kn = NA_KR * GRID_W
    ck = ck_ref[...].astype(BF16)
    cv = cv_ref[...].astype(BF16)

    def body(rb, carry):
        q0 = pl.multiple_of(rb * qn, qn)
        ks = jnp.clip(rb * NA_QR - NA_ROWS // 2, 0, n_blocks * NA_QR - NA_KR)
        k0 = pl.multiple_of(ks * GRID_W, GRID_W)
        ty = jnp.where(rb == 0, 0, jnp.where(rb == n_blocks - 1, 2, 1))
        q = q_ref[pl.ds(q0, qn), :].astype(BF16)
        k = k_ref[pl.ds(k0, kn), :].astype(BF16)
        v = v_ref[pl.ds(k0, kn), :].astype(BF16)
        s_loc = _dot_nt(q, k) * scale + bias_ref[ty]
        s_loc = jnp.where(mask_ref[ty] > 0.5, s_loc, NEG)
        s_ctx = _dot_nt(q, ck) * scale
        m = jnp.maximum(jnp.max(s_loc, axis=-1, keepdims=True), jnp.max(s_ctx, axis=-1, keepdims=True))
        p_loc = jnp.exp(s_loc - m)
        p_ctx = jnp.exp(s_ctx - m)
        l = jnp.sum(p_loc, axis=-1, keepdims=True) + jnp.sum(p_ctx, axis=-1, keepdims=True)
        o = (_dot(p_loc.astype(BF16), v) + _dot(p_ctx.astype(BF16), cv)) / l
        o_ref[pl.ds(q0, qn), :] = (o * _silu(z_ref[pl.ds(q0, qn), :])).astype(BF16)
        return carry

    lax.fori_loop(0, n_blocks, body, 0, unroll=2)


def _na_attn(proj, cache_k, cache_v, bias, mask, layer, nb, t):
    p_len = cache_k.shape[2]
    n_blocks = t // (NA_QR * GRID_W)

    def head(g):
        return pl.BlockSpec((None, t, LANE), lambda b, h: (g * CPG + h, b, 0))

    ctx = pl.BlockSpec((None, None, p_len, HEAD_DIM), lambda b, h: (b, layer, 0, h))
    return pl.pallas_call(
        functools.partial(_na_kernel, n_blocks=n_blocks),
        grid=(nb, H_A),
        in_specs=[head(G_QA), head(G_KA), head(G_VA), head(G_ZA), ctx, ctx,
                  pl.BlockSpec((None,) + bias.shape[1:], lambda b, h: (h, 0, 0, 0)),
                  pl.BlockSpec(mask.shape, lambda b, h: (0, 0, 0))],
        out_specs=pl.BlockSpec((t, HEAD_DIM), lambda b, h: (b, h)),
        out_shape=jax.ShapeDtypeStruct((nb * t, W_A), BF16),
        compiler_params=_params(("arbitrary", "arbitrary")),
        name="na_attn",
    )(proj, proj, proj, proj, cache_k, cache_v, bias, mask)


def _rope_tables(t):
    pos = np.arange(t)
    half = HEAD_DIM // 2
    inv = 1.0 / (ROPE_BASE ** (np.arange(0, half, 2, dtype=np.float64) / half))
    ang_r = (pos // GRID_W)[:, None] * inv[None, :]
    ang_c = (pos % GRID_W)[:, None] * inv[None, :]
    zero = np.zeros_like(ang_r)
    cos = np.concatenate([np.cos(ang_r), np.cos(ang_r), np.cos(ang_c), np.cos(ang_c)], axis=1)
    s_up = np.concatenate([-np.sin(ang_r), zero, -np.sin(ang_c), zero], axis=1)
    s_dn = np.concatenate([zero, np.sin(ang_r), zero, np.sin(ang_c)], axis=1)
    return tuple(jnp.asarray(a.astype(np.float32)) for a in (cos, s_up, s_dn))


def _band_mask(t):
    row = np.tile(np.arange(BAND), G_B)[:, None]
    col = np.arange(3 * BAND)[None, :]
    out = []
    for off in (0, -BAND, -2 * BAND):
        out.append(np.abs(off + col - row) <= BAND)
    return jnp.asarray(np.stack(out).astype(np.float32))


def _rope(x, cos, s_up, s_dn):
    quarter = HEAD_DIM // 4
    return x * cos + pltpu.roll(x, HEAD_DIM - quarter, 1) * s_up + pltpu.roll(x, quarter, 1) * s_dn


def _band_kernel(sink_ref, q_ref, k_ref, v_ref, z_ref, ck_ref, cv_ref, cos_ref, up_ref, dn_ref, mask_ref,
                 o_ref, kr_ref, vr_ref, *, n_blocks, blocks_per_step):
    scale = HEAD_DIM ** -0.5
    kv = pl.program_id(1)
    step = pl.program_id(2)
    t = n_blocks * BAND
    chunk = 512

    @pl.when(step == 0)
    def _():
        def prep(i, carry):
            r0 = pl.multiple_of(i * chunk, chunk)
            sl = pl.ds(r0, chunk)
            kr_ref[sl, :] = _rope(k_ref[sl, :], cos_ref[sl, :], up_ref[sl, :], dn_ref[sl, :]).astype(BF16)
            vr_ref[sl, :] = v_ref[sl, :].astype(BF16)
            return carry

        lax.fori_loop(0, t // chunk, prep, 0)

    ck = ck_ref[...].astype(BF16)
    cv = cv_ref[...].astype(BF16)
    sink = jnp.concatenate([jnp.full((BAND, 1), sink_ref[kv * G_B + g], F32) for g in range(G_B)], axis=0)

    def body(j, carry):
        i = step * blocks_per_step + j
        ql = pl.ds(pl.multiple_of(j * BAND, BAND), BAND)
        qs = pl.ds(pl.multiple_of(i * BAND, BAND), BAND)
        kb = jnp.clip(i - 1, 0, n_blocks - 3)
        k0 = pl.multiple_of(kb * BAND, BAND)
        ty = jnp.where(i == 0, 0, jnp.where(i == n_blocks - 1, 2, 1))
        cos = cos_ref[qs, :]
        up = up_ref[qs, :]
        dn = dn_ref[qs, :]
        q = jnp.concatenate([_rope(q_ref[g, ql, :], cos, up, dn).astype(BF16) for g in range(G_B)], axis=0)
        k = kr_ref[pl.ds(k0, 3 * BAND), :]
        v = vr_ref[pl.ds(k0, 3 * BAND), :]
        s_loc = jnp.where(mask_ref[ty] > 0.5, _dot_nt(q, k) * scale, NEG)
        s_ctx = _dot_nt(q, ck) * scale
        m = jnp.maximum(jnp.maximum(jnp.max(s_loc, axis=-1, keepdims=True),
                                    jnp.max(s_ctx, axis=-1, keepdims=True)), sink)
        p_loc = jnp.exp(s_loc - m)
        p_ctx = jnp.exp(s_ctx - m)
        l = jnp.sum(p_loc, axis=-1, keepdims=True) + jnp.sum(p_ctx, axis=-1, keepdims=True) + jnp.exp(sink - m)
        o = (_dot(p_loc.astype(BF16), v) + _dot(p_ctx.astype(BF16), cv)) / l
        for g in range(G_B):
            og = o[g * BAND:(g + 1) * BAND, :] * _silu(z_ref[g, ql, :])
            o_ref[ql, g * HEAD_DIM:(g + 1) * HEAD_DIM] = og.astype(BF16)
        return carry

    lax.fori_loop(0, blocks_per_step, body, 0, unroll=2)


def _band_attn(proj, sink, cache_k, cache_v, rope, mask, layer, nb, t, tq=1024):
    p_len = cache_k.shape[2]
    n_blocks = t // BAND
    nq = t // tq
    cos, s_up, s_dn = rope
    heads = pl.BlockSpec((G_B, tq, LANE), lambda b, kv, s: (G_QB * CPG // G_B + kv, b * nq + s, 0))
    gates = pl.BlockSpec((G_B, tq, LANE), lambda b, kv, s: (G_ZB * CPG // G_B + kv, b * nq + s, 0))
    keys = pl.BlockSpec((None, t, LANE), lambda b, kv, s: (C_KB + kv, b, 0))
    vals = pl.BlockSpec((None, t, LANE), lambda b, kv, s: (C_VB + kv, b, 0))
    ctx = pl.BlockSpec((None, None, p_len, HEAD_DIM), lambda b, kv, s: (b, layer, 0, kv))
    tab = pl.BlockSpec((t, HEAD_DIM), lambda b, kv, s: (0, 0))
    return pl.pallas_call(
        functools.partial(_band_kernel, n_blocks=n_blocks, blocks_per_step=tq // BAND),
        grid=(nb, KV_B, nq),
        in_specs=[pl.BlockSpec(memory_space=pltpu.SMEM), heads, keys, vals, gates, ctx, ctx, tab, tab, tab,
                  pl.BlockSpec(mask.shape, lambda b, kv, s: (0, 0, 0))],
        out_specs=pl.BlockSpec((tq, G_B * HEAD_DIM), lambda b, kv, s: (b * nq + s, kv)),
        out_shape=jax.ShapeDtypeStruct((nb * t, W_B), BF16),
        scratch_shapes=[pltpu.VMEM((t, HEAD_DIM), BF16), pltpu.VMEM((t, HEAD_DIM), BF16)],
        compiler_params=_params(("arbitrary", "arbitrary", "arbitrary")),
        name="band_attn",
    )(sink, proj, proj, proj, proj, cache_k, cache_v, cos, s_up, s_dn, mask)


def _cd_kernel(uc, ucp, ucn, zc, pa, pap, pan, pb, pbp, pbn, zd, wpool, pscale, cw, cb, lng, lnb, wpw, bpw,
               o_ref, ue, he, cv, hn, *, tt, t_seq):
    i = pl.program_id(1)
    has_prev = i > 0
    has_next = i < pl.num_programs(1) - 1
    ext = tt + 2 * HALO

    for c in range(CPG):
        ue[c, 0:HALO, :] = jnp.where(has_prev, ucp[c], 0.0)
        ue[c, HALO:HALO + tt, :] = uc[c]
        ue[c, HALO + tt:ext, :] = jnp.where(has_next, ucn[c], 0.0)
        he[c, 0:HALO, :] = jnp.where(has_prev, pap[c] * jax.nn.sigmoid(pbp[c]), 0.0)
        he[c, HALO:HALO + tt, :] = pa[c] * jax.nn.sigmoid(pb[c])
        he[c, HALO + tt:ext, :] = jnp.where(has_next, pan[c] * jax.nn.sigmoid(pbn[c]), 0.0)

    rt = 64
    for r0 in range(0, tt, rt):
        tpos = i * tt + r0 + lax.broadcasted_iota(jnp.int32, (rt, LANE), 0)
        for g, w in enumerate(POOL_WINDOWS):
            half = w // 2
            cnt = (jnp.minimum(tpos + half, t_seq) - jnp.maximum(tpos - half, 0)).astype(F32)
            d = []
            for c in (2 * g, 2 * g + 1):
                acc = ue[c, HALO + r0 - half:HALO + r0 - half + rt, :]
                for o in range(-half + 1, half):
                    acc = acc + ue[c, HALO + r0 + o:HALO + r0 + o + rt, :]
                d.append(acc / cnt - uc[c, r0:r0 + rt, :])
            dg = jnp.concatenate(d, axis=1).astype(BF16)
            cols = slice(g * POOL_GW, (g + 1) * POOL_GW)
            y = _dot(dg, wpool[g]) * pscale[:, cols]
            z = jnp.concatenate([zc[2 * g, r0:r0 + rt, :], zc[2 * g + 1, r0:r0 + rt, :]], axis=1)
            o_ref[r0:r0 + rt, cols] = (y * _silu(z)).astype(BF16)

    def conv_chunk(c, carry):
        wts = cw[c]
        bias = cb[c]
        for r0 in range(0, tt, rt):
            acc = jnp.zeros((rt, LANE), F32) + bias
            for k in range(CONV_K):
                off = HALO - CONV_K // 2 + k + r0
                acc = acc + he[c, off:off + rt, :] * wts[k:k + 1, :]
            cv[c, r0:r0 + rt, :] = acc
        return carry

    lax.fori_loop(0, CPG, conv_chunk, 0)

    def ln_rows(r, carry):
        r0 = pl.multiple_of(r * rt, rt)
        rs = pl.ds(r0, rt)
        xs = [cv[c, rs, :] for c in range(CPG)]
        tot = xs[0]
        for x in xs[1:]:
            tot = tot + x
        mu = jnp.sum(tot, axis=-1, keepdims=True) * (1.0 / W_D)
        sq = jnp.square(xs[0] - mu)
        for x in xs[1:]:
            sq = sq + jnp.square(x - mu)
        rstd = lax.rsqrt(jnp.sum(sq, axis=-1, keepdims=True) * (1.0 / W_D) + EPS)
        for c in range(CPG):
            cols = slice(c * LANE, (c + 1) * LANE)
            y = (xs[c] - mu) * rstd * lng[:, cols] + lnb[:, cols]
            hn[rs, cols] = _silu(y).astype(BF16)
        return carry

    lax.fori_loop(0, tt // rt, ln_rows, 0)

    for r0 in range(0, tt, 256):
        rs = slice(r0, r0 + 256)
        y = _dot(hn[rs, :], wpw[...]) + bpw[...]
        for c in range(CPG):
            cols = slice(c * LANE, (c + 1) * LANE)
            o_ref[rs, W_C + c * LANE:W_C + (c + 1) * LANE] = (y[:, cols] * _silu(zd[c, rs, :])).astype(BF16)


def _cd_mixers(proj, w_pool_bf, pool_scale, conv_w, conv_b, ln_g, ln_b, w_pw2_bf, b_pw2, nb, t, tt):
    m = nb * t
    nt = t // tt
    hb = tt // HALO
    last = m // HALO - 1

    def cur(g):
        return pl.BlockSpec((CPG, tt, LANE), lambda b, i: (g, b * nt + i, 0))

    def prev(g):
        return pl.BlockSpec((CPG, HALO, LANE), lambda b, i: (g, jnp.maximum((b * nt + i) * hb - 1, 0), 0))

    def nxt(g):
        return pl.BlockSpec((CPG, HALO, LANE), lambda b, i: (g, jnp.minimum((b * nt + i + 1) * hb, last), 0))

    def full(shape):
        return pl.BlockSpec(shape, lambda b, i: (0,) * len(shape))

    ext = tt + 2 * HALO
    cw = conv_w.reshape(CONV_K, CPG, LANE).transpose(1, 0, 2)
    cb = conv_b.reshape(CPG, 1, LANE)
    return pl.pallas_call(
        functools.partial(_cd_kernel, tt=tt, t_seq=t),
        grid=(nb, nt),
        in_specs=[cur(G_UC), prev(G_UC), nxt(G_UC), cur(G_ZC),
                  cur(G_PA), prev(G_PA), nxt(G_PA), cur(G_PB), prev(G_PB), nxt(G_PB), cur(G_ZD),
                  full((N_POOL, POOL_GW, POOL_GW)), full((1, W_C)), full((CPG, CONV_K, LANE)),
                  full((CPG, 1, LANE)), full((1, W_D)), full((1, W_D)), full((W_D, W_D)), full((1, W_D))],
        out_specs=pl.BlockSpec((tt, W_C + W_D), lambda b, i: (b * nt + i, 0)),
        out_shape=jax.ShapeDtypeStruct((m, W_C + W_D), BF16),
        scratch_shapes=[pltpu.VMEM((CPG, ext, LANE), F32), pltpu.VMEM((CPG, ext, LANE), F32),
                        pltpu.VMEM((CPG, tt, LANE), F32), pltpu.VMEM((tt, W_D), BF16)],
        compiler_params=_params(("arbitrary", "arbitrary")),
        name="cd_mixers",
    )(proj, proj, proj, proj, proj, proj, proj, proj, proj, proj, proj,
      w_pool_bf, pool_scale.reshape(1, W_C), cw, cb, ln_g.reshape(1, W_D), ln_b.reshape(1, W_D),
      w_pw2_bf, b_pw2.reshape(1, W_D))


def _outproj_kernel(*refs, tn, tm, last):
    if last:
        ya, yb, ycd, w_ref, x_ref, gate_ref, g_ref, y_ref, stage = refs
    else:
        ya, yb, ycd, w_ref, x_ref, gate_ref, g_ref, sh_ref, sc_ref, stage, h_ref = refs
    j = pl.program_id(1)
    acc = _dot(ya[...], w_ref[0:W_A, :])
    acc = acc + _dot(yb[...], w_ref[W_A:W_A + W_B, :])
    acc = acc + _dot(ycd[...], w_ref[W_A + W_B:, :])
    stage[:, pl.ds(pl.multiple_of(j * tn, tn), tn)] = x_ref[...] + gate_ref[...] * acc

    @pl.when(j == pl.num_programs(1) - 1)
    def _():
        g = g_ref[...]
        if last:
            def emit(rs, y):
                y_ref[rs, :] = y * g
        else:
            sc = 1.0 + sc_ref[...]
            sh = sh_ref[...]

            def emit(rs, y):
                h_ref[rs, :] = (y * g * sc + sh).astype(BF16)

        _norm_rows(stage, tm, emit)


def _outproj(ya, yb, ycd, w_bf, x, gate, norm, layer, tokens_per_cond, tm=512, tn=512):
    m = x.shape[0]
    tpc = tokens_per_cond // tm
    last = len(norm) == 1
    rows = pl.BlockSpec((tm, D_MODEL), lambda i, j: (i, 0))
    cond = pl.BlockSpec((None, 1, D_MODEL), lambda i, j: (i // tpc, 0, 0))
    in_specs = [
        pl.BlockSpec((tm, W_A), lambda i, j: (i, 0)),
        pl.BlockSpec((tm, W_B), lambda i, j: (i, 0)),
        pl.BlockSpec((tm, W_C + W_D), lambda i, j: (i, 0)),
        pl.BlockSpec((None, D_MODEL, tn), lambda i, j: (layer, 0, j)),
        pl.BlockSpec((tm, tn), lambda i, j: (i, j)),
        pl.BlockSpec((None, 1, tn), lambda i, j: (i // tpc, 0, j)),
        pl.BlockSpec((1, D_MODEL), lambda i, j: (0, 0)),
    ]
    args = [ya, yb, ycd, w_bf, x, gate, norm[0].reshape(1, D_MODEL)]
    if last:
        out_specs = rows
        out_shape = jax.ShapeDtypeStruct((m, D_MODEL), F32)
        scratch = [pltpu.VMEM((tm, D_MODEL), F32)]
    else:
        in_specs += [cond, cond]
        args += [norm[1], norm[2]]
        out_specs = [rows, rows]
        out_shape = [jax.ShapeDtypeStruct((m, D_MODEL), F32), jax.ShapeDtypeStruct((m, D_MODEL), BF16)]
        scratch = []
    return pl.pallas_call(
        functools.partial(_outproj_kernel, tn=tn, tm=tm, last=last),
        grid=(m // tm, D_MODEL // tn),
        in_specs=in_specs,
        out_specs=out_specs,
        out_shape=out_shape,
        scratch_shapes=scratch,
        compiler_params=_params(("arbitrary", "arbitrary")),
        name="outproj",
    )(*args)


def _permute_in_cols(w_in):
    k0 = 4 * W_A + W_B
    k1 = k0 + 2 * KV_B * HEAD_DIM
    return jnp.concatenate([w_in[..., :k0], w_in[..., k1:], w_in[..., k0:k1]], axis=-1)


def kernel(x_prompt, x_sample, cache_a_k, cache_a_v, cache_b_k, cache_b_v, c, c_ctx, norm_g, w_ada, b_ada, w_in,
           rpb_a, sink_b, w_pool, pool_scale, conv_w, conv_b, ln_g, ln_b, w_pw2, b_pw2, w_out, final_g):
    nbp, tp, _ = x_prompt.shape
    nbs, ts, _ = x_sample.shape
    p_len = cache_a_k.shape[2]
    assert ts == GRID_W * GRID_W and nbs + 1 <= 8

    w_in_bf = _permute_in_cols(w_in).astype(BF16)
    w_out_bf = w_out.astype(BF16)
    w_pool_bf = w_pool.astype(BF16)
    w_pw2_bf = w_pw2.astype(BF16)
    cak = cache_a_k.reshape(nbs, DEPTH, p_len, W_A)
    cav = cache_a_v.reshape(nbs, DEPTH, p_len, W_A)
    cbk = cache_b_k.reshape(nbs, DEPTH, p_len, KV_B * HEAD_DIM)
    cbv = cache_b_v.reshape(nbs, DEPTH, p_len, KV_B * HEAD_DIM)

    cond8 = jnp.concatenate([c_ctx[None, :], c, jnp.zeros((8 - 1 - nbs, D_MODEL), F32)], axis=0)
    ada = _ada(cond8, w_ada, b_ada)
    rope = _rope_tables(ts)
    band_mask = _band_mask(ts)

    def mod(l, k, lo, hi):
        return ada[l, lo:hi, k * D_MODEL:(k + 1) * D_MODEL].reshape(hi - lo, 1, D_MODEL)

    def next_norm(l, lo, hi):
        if l == DEPTH - 1:
            return (final_g,)
        return (norm_g[l + 1], mod(l + 1, 0, lo, hi), mod(l + 1, 1, lo, hi))

    xp = x_prompt.reshape(nbp * tp, D_MODEL)
    xs = x_sample.reshape(nbs * ts, D_MODEL)
    hp = _prenorm(xp, norm_g[0], mod(0, 0, 0, 1), mod(0, 1, 0, 1), nbp * tp)
    hs = _prenorm(xs, norm_g[0], mod(0, 0, 1, 1 + nbs), mod(0, 1, 1, 1 + nbs), ts)
    kv_out = [[], [], [], []]
    for l in range(DEPTH):
        na_bias, na_mask = _na_tables(rpb_a[l])

        proj = _inproj(hp, w_in_bf, l, tm=2048)
        ya, yb, nak, nav, nbk, nbv = _ctx_attn(proj, sink_b[l], nbp, tp)
        ycd = _cd_mixers(proj, w_pool_bf[l], pool_scale[l], conv_w[l], conv_b[l], ln_g[l], ln_b[l],
                         w_pw2_bf[l], b_pw2[l], nbp, tp, tt=tp)
        res = _outproj(ya, yb, ycd, w_out_bf, xp, mod(l, 2, 0, 1), next_norm(l, 0, 1), l, nbp * tp)
        if l == DEPTH - 1:
            y_prompt = res.reshape(nbp, tp, D_MODEL)
        else:
            xp, hp = res
        for lst, a in zip(kv_out, (nak, nav, nbk, nbv)):
            lst.append(a)

        proj = _inproj(hs, w_in_bf, l, tm=2048)
        ya = _na_attn(proj, cak, cav, na_bias, na_mask, l, nbs, ts)
        yb = _band_attn(proj, sink_b[l], cbk, cbv, rope, band_mask, l, nbs, ts)
        ycd = _cd_mixers(proj, w_pool_bf[l], pool_scale[l], conv_w[l], conv_b[l], ln_g[l], ln_b[l],
                         w_pw2_bf[l], b_pw2[l], nbs, ts, tt=512)
        res = _outproj(ya, yb, ycd, w_out_bf, xs, mod(l, 2, 1, 1 + nbs), next_norm(l, 1, 1 + nbs), l, ts)
        if l == DEPTH - 1:
            y_sample = res.reshape(nbs, ts, D_MODEL)
        else:
            xs, hs = res

    new_a_k = jnp.stack(kv_out[0], axis=1).reshape(nbp, DEPTH, tp, H_A, HEAD_DIM)
    new_a_v = jnp.stack(kv_out[1], axis=1).reshape(nbp, DEPTH, tp, H_A, HEAD_DIM)
    new_b_k = jnp.stack(kv_out[2], axis=1).reshape(nbp, DEPTH, tp, KV_B, HEAD_DIM)
    new_b_v = jnp.stack(kv_out[3], axis=1).reshape(nbp, DEPTH, tp, KV_B, HEAD_DIM)
    return (y_prompt, y_sample, new_a_k, new_a_v, new_b_k, new_b_v)
```

```python
import functools

import numpy as np
import jax
import jax.numpy as jnp
from jax import lax
from jax.experimental import pallas as pl
from jax.experimental.pallas import tpu as pltpu

D_MODEL = 4096
DEPTH = 2
GRID_W = 64
HEAD_DIM = 128
W_A = D_MODEL // 4
W_B = D_MODEL // 4
W_C = D_MODEL // 4
W_D = D_MODEL - W_A - W_B - W_C
H_A = W_A // HEAD_DIM
H_B = W_B // HEAD_DIM
KV_B = max(1, H_B // 4)
G_B = H_B // KV_B
NA_ROWS = 8
NA_COLS = 16
BAND = 128
N_POOL = 4
POOL_WINDOWS = (2, 4, 8, 16)
POOL_GW = W_C // N_POOL
CONV_K = 31
ROPE_BASE = 10000.0
EPS = 1e-6
NEG = -1e30
LOG2E = float(np.log2(np.e))
QK_SCALE_LOG2 = HEAD_DIM ** -0.5 * LOG2E
IN_W = 4 * W_A + 2 * W_B + 2 * KV_B * HEAD_DIM + 2 * W_C + 3 * W_D

LANE = 128
N_CHUNK = IN_W // LANE
CPG = W_A // LANE
G_QA, G_KA, G_VA, G_ZA, G_QB, G_ZB, G_UC, G_ZC, G_PA, G_PB, G_ZD = range(11)
C_KB = 11 * CPG
C_VB = C_KB + KV_B
HALO = 16
NA_QR = 4
NA_KR = 12
VMEM_LIMIT = 56 * 1024 * 1024

F32 = jnp.float32
BF16 = jnp.bfloat16


def _silu(x):
    return x * jax.nn.sigmoid(x)


def _dot(a, b):
    return jnp.dot(a, b, preferred_element_type=F32)


def _dot_nt(a, b):
    return lax.dot_general(a, b, (((1,), (1,)), ((), ())), preferred_element_type=F32)


def _params(sem):
    return pltpu.CompilerParams(dimension_semantics=sem, vmem_limit_bytes=VMEM_LIMIT)


def _ada_kernel(c_ref, w_ref, b_ref, o_ref):
    s = _silu(c_ref[...])
    o_ref[...] = _dot(s.astype(BF16), w_ref[...].astype(BF16)) + b_ref[...]


def _ada(cond8, w_ada, b_ada):
    tn = 512
    n = 3 * D_MODEL
    return pl.pallas_call(
        _ada_kernel,
        grid=(DEPTH, n // tn),
        in_specs=[
            pl.BlockSpec((8, D_MODEL), lambda l, j: (0, 0)),
            pl.BlockSpec((None, D_MODEL, tn), lambda l, j: (l, 0, j)),
            pl.BlockSpec((None, 1, tn), lambda l, j: (l, 0, j)),
        ],
        out_specs=pl.BlockSpec((None, 8, tn), lambda l, j: (l, 0, j)),
        out_shape=jax.ShapeDtypeStruct((DEPTH, 8, n), F32),
        compiler_params=_params(("arbitrary", "arbitrary")),
        name="ada",
    )(cond8, w_ada, b_ada.reshape(DEPTH, 1, n))


NORM_ROWS = 8


def _norm_rows(src_ref, rstd_ref, n_rows, emit):
    def tile(r):
        return pl.ds(pl.multiple_of(r * NORM_ROWS, NORM_ROWS), NORM_ROWS)

    def stats(r, carry):
        x = src_ref[tile(r), :]
        rstd_ref[tile(r), :] = lax.rsqrt(jnp.mean(x * x, axis=-1, keepdims=True) + EPS)
        return carry

    def apply(r, carry):
        emit(tile(r), src_ref[tile(r), :] * rstd_ref[tile(r), :])
        return carry

    lax.fori_loop(0, n_rows // NORM_ROWS, stats, 0, unroll=8)
    lax.fori_loop(0, n_rows // NORM_ROWS, apply, 0, unroll=4)


def _prenorm_kernel(x_ref, g_ref, sh_ref, sc_ref, h_ref, rstd_ref, *, tm):
    gs = jnp.broadcast_to(g_ref[...] * (1.0 + sc_ref[...]), (NORM_ROWS, D_MODEL))
    sh = jnp.broadcast_to(sh_ref[...], (NORM_ROWS, D_MODEL))

    def emit(rs, y):
        h_ref[rs, :] = (y * gs + sh).astype(BF16)

    _norm_rows(x_ref, rstd_ref, tm, emit)


def _prenorm(x, norm_g, shift, scale, tokens_per_cond, tm=256):
    m = x.shape[0]
    tpc = tokens_per_cond // tm
    cond = pl.BlockSpec((None, 1, D_MODEL), lambda i: (i // tpc, 0, 0))
    return pl.pallas_call(
        functools.partial(_prenorm_kernel, tm=tm),
        grid=(m // tm,),
        in_specs=[pl.BlockSpec((tm, D_MODEL), lambda i: (i, 0)), pl.BlockSpec((1, D_MODEL), lambda i: (0, 0)),
                  cond, cond],
        out_specs=pl.BlockSpec((tm, D_MODEL), lambda i: (i, 0)),
        out_shape=jax.ShapeDtypeStruct((m, D_MODEL), BF16),
        scratch_shapes=[pltpu.VMEM((tm, 1), F32)],
        compiler_params=_params(("arbitrary",)),
        name="prenorm",
    )(x, norm_g.reshape(1, D_MODEL), shift, scale)


def _inproj_kernel(h_ref, w_ref, o_ref, *, tn):
    acc = _dot(h_ref[...], w_ref[...])
    for s in range(tn // LANE):
        o_ref[s] = acc[:, s * LANE:(s + 1) * LANE]


def _inproj(h, w_bf, layer, tm, tn=512):
    m = h.shape[0]
    return pl.pallas_call(
        functools.partial(_inproj_kernel, tn=tn),
        grid=(m // tm, IN_W // tn),
        in_specs=[
            pl.BlockSpec((tm, D_MODEL), lambda i, j: (i, 0)),
            pl.BlockSpec((None, D_MODEL, tn), lambda i, j: (layer, 0, j)),
        ],
        out_specs=pl.BlockSpec((tn // LANE, tm, LANE), lambda i, j: (j, i, 0)),
        out_shape=jax.ShapeDtypeStruct((N_CHUNK, m, LANE), F32),
        compiler_params=_params(("arbitrary", "arbitrary")),
        name="inproj",
    )(h, w_bf)


def _ctx_attn_kernel(sink_ref, qa, ka, va, za, qb, zb, kb, vb, ya, yb, nak, nav, nbk, nbv):
    for h in range(H_A):
        cols = slice(h * HEAD_DIM, (h + 1) * HEAD_DIM)
        k32 = ka[h]
        v32 = va[h]
        s = _dot_nt((qa[h] * QK_SCALE_LOG2).astype(BF16), k32.astype(BF16))
        p = jnp.exp2(s - jnp.max(s, axis=-1, keepdims=True))
        o = _dot(p.astype(BF16), v32.astype(BF16)) / jnp.sum(p, axis=-1, keepdims=True)
        ya[:, cols] = (o * _silu(za[h])).astype(BF16)
        nak[:, cols] = k32
        nav[:, cols] = v32
    for kv in range(KV_B):
        cols = slice(kv * HEAD_DIM, (kv + 1) * HEAD_DIM)
        nbk[:, cols] = kb[kv]
        nbv[:, cols] = vb[kv]
    for h in range(H_B):
        cols = slice(h * HEAD_DIM, (h + 1) * HEAD_DIM)
        kv = h // G_B
        sink = sink_ref[h] * LOG2E
        s = _dot_nt((qb[h] * QK_SCALE_LOG2).astype(BF16), kb[kv].astype(BF16))
        m = jnp.maximum(jnp.max(s, axis=-1, keepdims=True), sink)
        p = jnp.exp2(s - m)
        l = jnp.sum(p, axis=-1, keepdims=True) + jnp.exp2(sink - m)
        o = _dot(p.astype(BF16), vb[kv].astype(BF16)) / l
        yb[:, cols] = (o * _silu(zb[h])).astype(BF16)


def _ctx_attn(proj, sink, nb, t):
    m = nb * t

    def grp(g):
        return pl.BlockSpec((CPG, t, LANE), lambda b: (g, b, 0))

    def kvs(c):
        return pl.BlockSpec((KV_B, t, LANE), lambda b: (c // KV_B, b, 0))

    wide = pl.BlockSpec((t, W_A), lambda b: (b, 0))
    wide3 = pl.BlockSpec((None, t, W_A), lambda b: (b, 0, 0))
    narrow3 = pl.BlockSpec((None, t, KV_B * HEAD_DIM), lambda b: (b, 0, 0))
    return pl.pallas_call(
        _ctx_attn_kernel,
        grid=(nb,),
        in_specs=[pl.BlockSpec(memory_space=pltpu.SMEM),
                  grp(G_QA), grp(G_KA), grp(G_VA), grp(G_ZA), grp(G_QB), grp(G_ZB), kvs(C_KB), kvs(C_VB)],
        out_specs=[wide, wide, wide3, wide3, narrow3, narrow3],
        out_shape=[jax.ShapeDtypeStruct((m, W_A), BF16), jax.ShapeDtypeStruct((m, W_B), BF16),
                   jax.ShapeDtypeStruct((nb, t, W_A), F32), jax.ShapeDtypeStruct((nb, t, W_A), F32),
                   jax.ShapeDtypeStruct((nb, t, KV_B * HEAD_DIM), F32),
                   jax.ShapeDtypeStruct((nb, t, KV_B * HEAD_DIM), F32)],
        compiler_params=_params(("arbitrary",)),
        name="ctx_attn",
    )(sink, proj, proj, proj, proj, proj, proj, proj, proj)


def _na_tables(rpb):
    rows = GRID_W
    n_dr = 2 * NA_ROWS - 1
    n_dc = 2 * NA_COLS - 1
    c = np.arange(GRID_W)[:, None]
    kc = np.arange(GRID_W)[None, :]
    start = np.clip(c - NA_COLS // 2, 0, GRID_W - NA_COLS)
    col_ok = (kc >= start) & (kc < start + NA_COLS)
    dc = np.clip(kc - c + NA_COLS - 1, 0, n_dc - 1)
    onehot = (dc.reshape(1, -1) == np.arange(n_dc)[:, None]).astype(np.float32)
    toep = jnp.einsum("hrd,dx->hrx", rpb, jnp.asarray(onehot), precision=lax.Precision.HIGHEST)
    toep = toep.reshape(H_A, n_dr, GRID_W, GRID_W)
    padded = jnp.concatenate([toep[:, :1]] * NA_KR + [toep] + [toep[:, -1:]] * NA_KR, axis=1)
    i = np.arange(NA_QR)[:, None]
    j = np.arange(NA_KR)[None, :]
    masks, slabs = [], []
    for r0, ks in ((0, 0), (NA_QR, 0), (rows - NA_QR, rows - NA_KR)):
        r = r0 + i
        rs = np.clip(r - NA_ROWS // 2, 0, rows - NA_ROWS)
        kr = ks + j
        row_ok = (kr >= rs) & (kr < rs + NA_ROWS)
        masks.append(row_ok[:, None, :, None] & col_ok[None, :, None, :])
        for qi in range(NA_QR):
            dr0 = ks - (r0 + qi) + NA_ROWS - 1 + NA_KR
            slabs.append(padded[:, dr0:dr0 + NA_KR])
    flat = (NA_QR * GRID_W, NA_KR * GRID_W)
    mask = np.stack([mk.reshape(flat) for mk in masks]).astype(np.float32)
    bias = jnp.stack(slabs, axis=1).reshape(H_A, 3, NA_QR, NA_KR, GRID_W, GRID_W)
    bias = bias.transpose(0, 1, 2, 4, 3, 5).reshape((H_A, 3) + flat) * LOG2E
    return bias, jnp.asarray(mask)


def _na_kernel(q_ref, k_ref, v_ref, z_ref, ck_ref, cv_ref, bias_ref, mask_ref, o_ref, *, n_blocks):
    qn = NA_QR * GRID_W
    kn = NA_KR * GRID_W
    ck = ck_ref[...].astype(BF16)
    cv = cv_ref[...].astype(BF16)

    def body(rb, carry):
        q0 = pl.multiple_of(rb * qn, qn)
        ks = jnp.clip(rb * NA_QR - NA_ROWS // 2, 0, n_blocks * NA_QR - NA_KR)
        k0 = pl.multiple_of(ks * GRID_W, GRID_W)
        ty = jnp.where(rb == 0, 0, jnp.where(rb == n_blocks - 1, 2, 1))
        q = (q_ref[pl.ds(q0, qn), :] * QK_SCALE_LOG2).astype(BF16)
        k = k_ref[pl.ds(k0, kn), :].astype(BF16)
        v = v_ref[pl.ds(k0, kn), :].astype(BF16)
        s_loc = _dot_nt(q, k) + bias_ref[ty]
        s_loc = jnp.where(mask_ref[ty] > 0.5, s_loc, NEG)
        s_ctx = _dot_nt(q, ck)
        m = jnp.maximum(jnp.max(s_loc, axis=-1, keepdims=True), jnp.max(s_ctx, axis=-1, keepdims=True))
        p_loc = jnp.exp2(s_loc - m)
        p_ctx = jnp.exp2(s_ctx - m)
        l = jnp.sum(p_loc, axis=-1, keepdims=True) + jnp.sum(p_ctx, axis=-1, keepdims=True)
        o = (_dot(p_loc.astype(BF16), v) + _dot(p_ctx.astype(BF16), cv)) / l
        o_ref[pl.ds(q0, qn), :] = (o * _silu(z_ref[pl.ds(q0, qn), :])).astype(BF16)
        return carry

    lax.fori_loop(0, n_blocks, body, 0, unroll=2)


def _na_attn(proj, cache_k, cache_v, bias, mask, layer, nb, t):
    p_len = cache_k.shape[2]
    n_blocks = t // (NA_QR * GRID_W)

    def head(g):
        return pl.BlockSpec((None, t, LANE), lambda b, h: (g * CPG + h, b, 0))

    ctx = pl.BlockSpec((None, None, p_len, HEAD_DIM), lambda b, h: (b, layer, 0, h))
    return pl.pallas_call(
        functools.partial(_na_kernel, n_blocks=n_blocks),
        grid=(nb, H_A),
        in_specs=[head(G_QA), head(G_KA), head(G_VA), head(G_ZA), ctx, ctx,
                  pl.BlockSpec((None,) + bias.shape[1:], lambda b, h: (h, 0, 0, 0)),
                  pl.BlockSpec(mask.shape, lambda b, h: (0, 0, 0))],
        out_specs=pl.BlockSpec((t, HEAD_DIM), lambda b, h: (b, h)),
        out_shape=jax.ShapeDtypeStruct((nb * t, W_A), BF16),
        compiler_params=_params(("arbitrary", "arbitrary")),
        name="na_attn",
    )(proj, proj, proj, proj, cache_k, cache_v, bias, mask)


def _rope_tables(t):
    pos = np.arange(t)
    half = HEAD_DIM // 2
    inv = 1.0 / (ROPE_BASE ** (np.arange(0, half, 2, dtype=np.float64) / half))
    ang_r = (pos // GRID_W)[:, None] * inv[None, :]
    ang_c = (pos % GRID_W)[:, None] * inv[None, :]
    zero = np.zeros_like(ang_r)
    cos = np.concatenate([np.cos(ang_r), np.cos(ang_r), np.cos(ang_c), np.cos(ang_c)], axis=1)
    s_up = np.concatenate([-np.sin(ang_r), zero, -np.sin(ang_c), zero], axis=1)
    s_dn = np.concatenate([zero, np.sin(ang_r), zero, np.sin(ang_c)], axis=1)
    return tuple(jnp.asarray(a.astype(np.float32)) for a in (cos, s_up, s_dn))


def _band_mask(t):
    row = np.tile(np.arange(BAND), G_B)[:, None]
    col = np.arange(3 * BAND)[None, :]
    out = []
    for off in (0, -BAND, -2 * BAND):
        out.append(np.abs(off + col - row) <= BAND)
    return jnp.asarray(np.stack(out).astype(np.float32))


def _rope(x, cos, s_up, s_dn):
    quarter = HEAD_DIM // 4
    return x * cos + pltpu.roll(x, HEAD_DIM - quarter, 1) * s_up + pltpu.roll(x, quarter, 1) * s_dn


def _band_kernel(sink_ref, q_ref, k_ref, v_ref, z_ref, ck_ref, cv_ref, cos_ref, up_ref, dn_ref, mask_ref,
                 o_ref, kr_ref, vr_ref, *, n_blocks, blocks_per_step):
    kv = pl.program_id(1)
    step = pl.program_id(2)
    t = n_blocks * BAND
    chunk = 512

    @pl.when(step == 0)
    def _():
        def prep(i, carry):
            r0 = pl.multiple_of(i * chunk, chunk)
            sl = pl.ds(r0, chunk)
            kr_ref[sl, :] = _rope(k_ref[sl, :], cos_ref[sl, :], up_ref[sl, :], dn_ref[sl, :]).astype(BF16)
            vr_ref[sl, :] = v_ref[sl, :].astype(BF16)
            return carry

        lax.fori_loop(0, t // chunk, prep, 0)

    ck = ck_ref[...].astype(BF16)
    cv = cv_ref[...].astype(BF16)
    sink = jnp.concatenate([jnp.full((BAND, 1), sink_ref[kv * G_B + g] * LOG2E, F32) for g in range(G_B)], axis=0)

    def body(j, carry):
        i = step * blocks_per_step + j
        ql = pl.ds(pl.multiple_of(j * BAND, BAND), BAND)
        qs = pl.ds(pl.multiple_of(i * BAND, BAND), BAND)
        kb = jnp.clip(i - 1, 0, n_blocks - 3)
        k0 = pl.multiple_of(kb * BAND, BAND)
        ty = jnp.where(i == 0, 0, jnp.where(i == n_blocks - 1, 2, 1))
        cos = cos_ref[qs, :]
        up = up_ref[qs, :]
        dn = dn_ref[qs, :]
        q = jnp.concatenate([(_rope(q_ref[g, ql, :], cos, up, dn) * QK_SCALE_LOG2).astype(BF16)
                             for g in range(G_B)], axis=0)
        k = kr_ref[pl.ds(k0, 3 * BAND), :]
        v = vr_ref[pl.ds(k0, 3 * BAND), :]
        s_loc = jnp.where(mask_ref[ty] > 0.5, _dot_nt(q, k), NEG)
        s_ctx = _dot_nt(q, ck)
        m = jnp.maximum(jnp.maximum(jnp.max(s_loc, axis=-1, keepdims=True),
                                    jnp.max(s_ctx, axis=-1, keepdims=True)), sink)
        p_loc = jnp.exp2(s_loc - m)
        p_ctx = jnp.exp2(s_ctx - m)
        l = jnp.sum(p_loc, axis=-1, keepdims=True) + jnp.sum(p_ctx, axis=-1, keepdims=True) + jnp.exp2(sink - m)
        o = (_dot(p_loc.astype(BF16), v) + _dot(p_ctx.astype(BF16), cv)) / l
        for g in range(G_B):
            og = o[g * BAND:(g + 1) * BAND, :] * _silu(z_ref[g, ql, :])
            o_ref[ql, g * HEAD_DIM:(g + 1) * HEAD_DIM] = og.astype(BF16)
        return carry

    lax.fori_loop(0, blocks_per_step, body, 0, unroll=2)


def _band_attn(proj, sink, cache_k, cache_v, rope, mask, layer, nb, t, tq=1024):
    p_len = cache_k.shape[2]
    n_blocks = t // BAND
    nq = t // tq
    cos, s_up, s_dn = rope
    heads = pl.BlockSpec((G_B, tq, LANE), lambda b, kv, s: (G_QB * CPG // G_B + kv, b * nq + s, 0))
    gates = pl.BlockSpec((G_B, tq, LANE), lambda b, kv, s: (G_ZB * CPG // G_B + kv, b * nq + s, 0))
    keys = pl.BlockSpec((None, t, LANE), lambda b, kv, s: (C_KB + kv, b, 0))
    vals = pl.BlockSpec((None, t, LANE), lambda b, kv, s: (C_VB + kv, b, 0))
    ctx = pl.BlockSpec((None, None, p_len, HEAD_DIM), lambda b, kv, s: (b, layer, 0, kv))
    tab = pl.BlockSpec((t, HEAD_DIM), lambda b, kv, s: (0, 0))
    return pl.pallas_call(
        functools.partial(_band_kernel, n_blocks=n_blocks, blocks_per_step=tq // BAND),
        grid=(nb, KV_B, nq),
        in_specs=[pl.BlockSpec(memory_space=pltpu.SMEM), heads, keys, vals, gates, ctx, ctx, tab, tab, tab,
                  pl.BlockSpec(mask.shape, lambda b, kv, s: (0, 0, 0))],
        out_specs=pl.BlockSpec((tq, G_B * HEAD_DIM), lambda b, kv, s: (b * nq + s, kv)),
        out_shape=jax.ShapeDtypeStruct((nb * t, W_B), BF16),
        scratch_shapes=[pltpu.VMEM((t, HEAD_DIM), BF16), pltpu.VMEM((t, HEAD_DIM), BF16)],
        compiler_params=_params(("arbitrary", "arbitrary", "arbitrary")),
        name="band_attn",
    )(sink, proj, proj, proj, proj, cache_k, cache_v, cos, s_up, s_dn, mask)


def _cd_kernel(uc, ucp, ucn, zc, pa, pap, pan, pb, pbp, pbn, zd, wpool, pscale, cw, cb, lng, lnb, wpw, bpw,
               o_ref, ue, he, cv, hn, *, tt, t_seq):
    i = pl.program_id(1)
    has_prev = i > 0
    has_next = i < pl.num_programs(1) - 1
    ext = tt + 2 * HALO

    for c in range(CPG):
        ue[c, 0:HALO, :] = jnp.where(has_prev, ucp[c], 0.0)
        ue[c, HALO:HALO + tt, :] = uc[c]
        ue[c, HALO + tt:ext, :] = jnp.where(has_next, ucn[c], 0.0)
        he[c, 0:HALO, :] = jnp.where(has_prev, pap[c] * jax.nn.sigmoid(pbp[c]), 0.0)
        he[c, HALO:HALO + tt, :] = pa[c] * jax.nn.sigmoid(pb[c])
        he[c, HALO + tt:ext, :] = jnp.where(has_next, pan[c] * jax.nn.sigmoid(pbn[c]), 0.0)

    rt = 64
    for r0 in range(0, tt, rt):
        tpos = i * tt + r0 + lax.broadcasted_iota(jnp.int32, (rt, LANE), 0)
        for g, w in enumerate(POOL_WINDOWS):
            half = w // 2
            cnt = (jnp.minimum(tpos + half, t_seq) - jnp.maximum(tpos - half, 0)).astype(F32)
            d = []
            for c in (2 * g, 2 * g + 1):
                acc = ue[c, HALO + r0 - half:HALO + r0 - half + rt, :]
                for o in range(-half + 1, half):
                    acc = acc + ue[c, HALO + r0 + o:HALO + r0 + o + rt, :]
                d.append(acc / cnt - uc[c, r0:r0 + rt, :])
            dg = jnp.concatenate(d, axis=1).astype(BF16)
            cols = slice(g * POOL_GW, (g + 1) * POOL_GW)
            y = _dot(dg, wpool[g]) * pscale[:, cols]
            z = jnp.concatenate([zc[2 * g, r0:r0 + rt, :], zc[2 * g + 1, r0:r0 + rt, :]], axis=1)
            o_ref[r0:r0 + rt, cols] = (y * _silu(z)).astype(BF16)

    def conv_chunk(c, carry):
        wts = cw[c]
        bias = cb[c]
        for r0 in range(0, tt, rt):
            acc = jnp.zeros((rt, LANE), F32) + bias
            for k in range(CONV_K):
                off = HALO - CONV_K // 2 + k + r0
                acc = acc + he[c, off:off + rt, :] * wts[k:k + 1, :]
            cv[c, r0:r0 + rt, :] = acc
        return carry

    lax.fori_loop(0, CPG, conv_chunk, 0)

    def ln_rows(r, carry):
        r0 = pl.multiple_of(r * rt, rt)
        rs = pl.ds(r0, rt)
        xs = [cv[c, rs, :] for c in range(CPG)]
        tot = xs[0]
        for x in xs[1:]:
            tot = tot + x
        mu = jnp.sum(tot, axis=-1, keepdims=True) * (1.0 / W_D)
        sq = jnp.square(xs[0] - mu)
        for x in xs[1:]:
            sq = sq + jnp.square(x - mu)
        rstd = lax.rsqrt(jnp.sum(sq, axis=-1, keepdims=True) * (1.0 / W_D) + EPS)
        for c in range(CPG):
            cols = slice(c * LANE, (c + 1) * LANE)
            y = (xs[c] - mu) * rstd * lng[:, cols] + lnb[:, cols]
            hn[rs, cols] = _silu(y).astype(BF16)
        return carry

    lax.fori_loop(0, tt // rt, ln_rows, 0, unroll=2)

    for r0 in range(0, tt, 256):
        rs = slice(r0, r0 + 256)
        y = _dot(hn[rs, :], wpw[...]) + bpw[...]
        for c in range(CPG):
            cols = slice(c * LANE, (c + 1) * LANE)
            o_ref[rs, W_C + c * LANE:W_C + (c + 1) * LANE] = (y[:, cols] * _silu(zd[c, rs, :])).astype(BF16)


def _cd_mixers(proj, w_pool_bf, pool_scale, conv_w, conv_b, ln_g, ln_b, w_pw2_bf, b_pw2, nb, t, tt):
    m = nb * t
    nt = t // tt
    hb = tt // HALO
    last = m // HALO - 1

    def cur(g):
        return pl.BlockSpec((CPG, tt, LANE), lambda b, i: (g, b * nt + i, 0))

    def prev(g):
        return pl.BlockSpec((CPG, HALO, LANE), lambda b, i: (g, jnp.maximum((b * nt + i) * hb - 1, 0), 0))

    def nxt(g):
        return pl.BlockSpec((CPG, HALO, LANE), lambda b, i: (g, jnp.minimum((b * nt + i + 1) * hb, last), 0))

    def full(shape):
        return pl.BlockSpec(shape, lambda b, i: (0,) * len(shape))

    ext = tt + 2 * HALO
    cw = conv_w.reshape(CONV_K, CPG, LANE).transpose(1, 0, 2)
    cb = conv_b.reshape(CPG, 1, LANE)
    return pl.pallas_call(
        functools.partial(_cd_kernel, tt=tt, t_seq=t),
        grid=(nb, nt),
        in_specs=[cur(G_UC), prev(G_UC), nxt(G_UC), cur(G_ZC),
                  cur(G_PA), prev(G_PA), nxt(G_PA), cur(G_PB), prev(G_PB), nxt(G_PB), cur(G_ZD),
                  full((N_POOL, POOL_GW, POOL_GW)), full((1, W_C)), full((CPG, CONV_K, LANE)),
                  full((CPG, 1, LANE)), full((1, W_D)), full((1, W_D)), full((W_D, W_D)), full((1, W_D))],
        out_specs=pl.BlockSpec((tt, W_C + W_D), lambda b, i: (b * nt + i, 0)),
        out_shape=jax.ShapeDtypeStruct((m, W_C + W_D), BF16),
        scratch_shapes=[pltpu.VMEM((CPG, ext, LANE), F32), pltpu.VMEM((CPG, ext, LANE), F32),
                        pltpu.VMEM((CPG, tt, LANE), F32), pltpu.VMEM((tt, W_D), BF16)],
        compiler_params=_params(("arbitrary", "arbitrary")),
        name="cd_mixers",
    )(proj, proj, proj, proj, proj, proj, proj, proj, proj, proj, proj,
      w_pool_bf, pool_scale.reshape(1, W_C), cw, cb, ln_g.reshape(1, W_D), ln_b.reshape(1, W_D),
      w_pw2_bf, b_pw2.reshape(1, W_D))


def _outproj_kernel(*refs, tn, tm, last):
    if last:
        ya, yb, ycd, w_ref, x_ref, gate_ref, g_ref, y_ref, stage, rstd_ref = refs
    else:
        ya, yb, ycd, w_ref, x_ref, gate_ref, g_ref, sh_ref, sc_ref, stage, h_ref, rstd_ref = refs
    j = pl.program_id(1)
    acc = _dot(ya[...], w_ref[0:W_A, :])
    acc = acc + _dot(yb[...], w_ref[W_A:W_A + W_B, :])
    acc = acc + _dot(ycd[...], w_ref[W_A + W_B:, :])
    stage[:, pl.ds(pl.multiple_of(j * tn, tn), tn)] = x_ref[...] + gate_ref[...] * acc

    @pl.when(j == pl.num_programs(1) - 1)
    def _():
        if last:
            g = jnp.broadcast_to(g_ref[...], (NORM_ROWS, D_MODEL))

            def emit(rs, y):
                y_ref[rs, :] = y * g
        else:
            gs = jnp.broadcast_to(g_ref[...] * (1.0 + sc_ref[...]), (NORM_ROWS, D_MODEL))
            sh = jnp.broadcast_to(sh_ref[...], (NORM_ROWS, D_MODEL))

            def emit(rs, y):
                h_ref[rs, :] = (y * gs + sh).astype(BF16)

        _norm_rows(stage, rstd_ref, tm, emit)


def _outproj(ya, yb, ycd, w_bf, x, gate, norm, layer, tokens_per_cond, tm=512, tn=1024):
    m = x.shape[0]
    tpc = tokens_per_cond // tm
    last = len(norm) == 1
    rows = pl.BlockSpec((tm, D_MODEL), lambda i, j: (i, 0))
    cond = pl.BlockSpec((None, 1, D_MODEL), lambda i, j: (i // tpc, 0, 0))
    in_specs = [
        pl.BlockSpec((tm, W_A), lambda i, j: (i, 0)),
        pl.BlockSpec((tm, W_B), lambda i, j: (i, 0)),
        pl.BlockSpec((tm, W_C + W_D), lambda i, j: (i, 0)),
        pl.BlockSpec((None, D_MODEL, tn), lambda i, j: (layer, 0, j)),
        pl.BlockSpec((tm, tn), lambda i, j: (i, j)),
        pl.BlockSpec((None, 1, tn), lambda i, j: (i // tpc, 0, j)),
        pl.BlockSpec((1, D_MODEL), lambda i, j: (0, 0)),
    ]
    args = [ya, yb, ycd, w_bf, x, gate, norm[0].reshape(1, D_MODEL)]
    rstd = pltpu.VMEM((tm, 1), F32)
    if last:
        out_specs = rows
        out_shape = jax.ShapeDtypeStruct((m, D_MODEL), F32)
        scratch = [pltpu.VMEM((tm, D_MODEL), F32), rstd]
    else:
        in_specs += [cond, cond]
        args += [norm[1], norm[2]]
        out_specs = [rows, rows]
        out_shape = [jax.ShapeDtypeStruct((m, D_MODEL), F32), jax.ShapeDtypeStruct((m, D_MODEL), BF16)]
        scratch = [rstd]
    return pl.pallas_call(
        functools.partial(_outproj_kernel, tn=tn, tm=tm, last=last),
        grid=(m // tm, D_MODEL // tn),
        in_specs=in_specs,
        out_specs=out_specs,
        out_shape=out_shape,
        scratch_shapes=scratch,
        compiler_params=_params(("arbitrary", "arbitrary")),
        name="outproj",
    )(*args)


def _permute_in_cols(w_in):
    k0 = 4 * W_A + W_B
    k1 = k0 + 2 * KV_B * HEAD_DIM
    return jnp.concatenate([w_in[..., :k0], w_in[..., k1:], w_in[..., k0:k1]], axis=-1)


def kernel(x_prompt, x_sample, cache_a_k, cache_a_v, cache_b_k, cache_b_v, c, c_ctx, norm_g, w_ada, b_ada, w_in,
           rpb_a, sink_b, w_pool, pool_scale, conv_w, conv_b, ln_g, ln_b, w_pw2, b_pw2, w_out, final_g):
    nbp, tp, _ = x_prompt.shape
    nbs, ts, _ = x_sample.shape
    p_len = cache_a_k.shape[2]
    assert ts == GRID_W * GRID_W and nbs + 1 <= 8

    w_in_bf = _permute_in_cols(w_in).astype(BF16)
    w_out_bf = w_out.astype(BF16)
    w_pool_bf = w_pool.astype(BF16)
    w_pw2_bf = w_pw2.astype(BF16)
    cak = cache_a_k.reshape(nbs, DEPTH, p_len, W_A)
    cav = cache_a_v.reshape(nbs, DEPTH, p_len, W_A)
    cbk = cache_b_k.reshape(nbs, DEPTH, p_len, KV_B * HEAD_DIM)
    cbv = cache_b_v.reshape(nbs, DEPTH, p_len, KV_B * HEAD_DIM)

    cond8 = jnp.concatenate([c_ctx[None, :], c, jnp.zeros((8 - 1 - nbs, D_MODEL), F32)], axis=0)
    ada = _ada(cond8, w_ada, b_ada)
    rope = _rope_tables(ts)
    band_mask = _band_mask(ts)

    def mod(l, k, lo, hi):
        return ada[l, lo:hi, k * D_MODEL:(k + 1) * D_MODEL].reshape(hi - lo, 1, D_MODEL)

    def next_norm(l, lo, hi):
        if l == DEPTH - 1:
            return (final_g,)
        return (norm_g[l + 1], mod(l + 1, 0, lo, hi), mod(l + 1, 1, lo, hi))

    xp = x_prompt.reshape(nbp * tp, D_MODEL)
    xs = x_sample.reshape(nbs * ts, D_MODEL)
    hp = _prenorm(xp, norm_g[0], mod(0, 0, 0, 1), mod(0, 1, 0, 1), nbp * tp)
    hs = _prenorm(xs, norm_g[0], mod(0, 0, 1, 1 + nbs), mod(0, 1, 1, 1 + nbs), ts)
    kv_out = [[], [], [], []]
    for l in range(DEPTH):
        na_bias, na_mask = _na_tables(rpb_a[l])

        proj = _inproj(hp, w_in_bf, l, tm=2048)
        ya, yb, nak, nav, nbk, nbv = _ctx_attn(proj, sink_b[l], nbp, tp)
        ycd = _cd_mixers(proj, w_pool_bf[l], pool_scale[l], conv_w[l], conv_b[l], ln_g[l], ln_b[l],
                         w_pw2_bf[l], b_pw2[l], nbp, tp, tt=tp)
        res = _outproj(ya, yb, ycd, w_out_bf, xp, mod(l, 2, 0, 1), next_norm(l, 0, 1), l, nbp * tp)
        if l == DEPTH - 1:
            y_prompt = res.reshape(nbp, tp, D_MODEL)
        else:
            xp, hp = res
        for lst, a in zip(kv_out, (nak, nav, nbk, nbv)):
            lst.append(a)

        proj = _inproj(hs, w_in_bf, l, tm=2048)
        ya = _na_attn(proj, cak, cav, na_bias, na_mask, l, nbs, ts)
        yb = _band_attn(proj, sink_b[l], cbk, cbv, rope, band_mask, l, nbs, ts)
        ycd = _cd_mixers(proj, w_pool_bf[l], pool_scale[l], conv_w[l], conv_b[l], ln_g[l], ln_b[l],
                         w_pw2_bf[l], b_pw2[l], nbs, ts, tt=512)
        res = _outproj(ya, yb, ycd, w_out_bf, xs, mod(l, 2, 1, 1 + nbs), next_norm(l, 1, 1 + nbs), l, ts)
        if l == DEPTH - 1:
            y_sample = res.reshape(nbs, ts, D_MODEL)
        else:
            xs, hs = res

    new_a_k = jnp.stack(kv_out[0], axis=1).reshape(nbp, DEPTH, tp, H_A, HEAD_DIM)
    new_a_v = jnp.stack(kv_out[1], axis=1).reshape(nbp, DEPTH, tp, H_A, HEAD_DIM)
    new_b_k = jnp.stack(kv_out[2], axis=1).reshape(nbp, DEPTH, tp, KV_B, HEAD_DIM)
    new_b_v = jnp.stack(kv_out[3], axis=1).reshape(nbp, DEPTH, tp, KV_B, HEAD_DIM)
    return (y_prompt, y_sample, new_a_k, new_a_v, new_b_k, new_b_v)
```

```python
import functools

import numpy as np
import jax
import jax.numpy as jnp
from jax import lax
from jax.experimental import pallas as pl
from jax.experimental.pallas import tpu as pltpu

D_MODEL = 4096
DEPTH = 2
GRID_W = 64
HEAD_DIM = 128
W_A = D_MODEL // 4
W_B = D_MODEL // 4
W_C = D_MODEL // 4
W_D = D_MODEL - W_A - W_B - W_C
H_A = W_A // HEAD_DIM
H_B = W_B // HEAD_DIM
KV_B = max(1, H_B // 4)
G_B = H_B // KV_B
NA_ROWS = 8
NA_COLS = 16
BAND = 128
N_POOL = 4
POOL_WINDOWS = (2, 4, 8, 16)
POOL_GW = W_C // N_POOL
CONV_K = 31
ROPE_BASE = 10000.0
EPS = 1e-6
NEG = -1e30
LOG2E = float(np.log2(np.e))
QK_SCALE_LOG2 = HEAD_DIM ** -0.5 * LOG2E
IN_W = 4 * W_A + 2 * W_B + 2 * KV_B * HEAD_DIM + 2 * W_C + 3 * W_D

LANE = 128
N_CHUNK = IN_W // LANE
CPG = W_A // LANE
G_QA, G_KA, G_VA, G_ZA, G_QB, G_ZB, G_UC, G_ZC, G_PA, G_PB, G_ZD = range(11)
C_KB = 11 * CPG
C_VB = C_KB + KV_B
HALO = 16
NA_QR = 4
NA_KR = 12
VMEM_LIMIT = 56 * 1024 * 1024

F32 = jnp.float32
BF16 = jnp.bfloat16


def _silu(x):
    return x * jax.nn.sigmoid(x)


def _dot(a, b):
    return jnp.dot(a, b, preferred_element_type=F32)


def _dot_nt(a, b):
    return lax.dot_general(a, b, (((1,), (1,)), ((), ())), preferred_element_type=F32)


def _params(sem):
    return pltpu.CompilerParams(dimension_semantics=sem, vmem_limit_bytes=VMEM_LIMIT)


def _ada_kernel(c_ref, w_ref, b_ref, o_ref):
    s = _silu(c_ref[...])
    o_ref[...] = _dot(s.astype(BF16), w_ref[...].astype(BF16)) + b_ref[...]


def _ada(cond8, w_ada, b_ada):
    tn = 512
    n = 3 * D_MODEL
    return pl.pallas_call(
        _ada_kernel,
        grid=(DEPTH, n // tn),
        in_specs=[
            pl.BlockSpec((8, D_MODEL), lambda l, j: (0, 0)),
            pl.BlockSpec((None, D_MODEL, tn), lambda l, j: (l, 0, j)),
            pl.BlockSpec((None, 1, tn), lambda l, j: (l, 0, j)),
        ],
        out_specs=pl.BlockSpec((None, 8, tn), lambda l, j: (l, 0, j)),
        out_shape=jax.ShapeDtypeStruct((DEPTH, 8, n), F32),
        compiler_params=_params(("arbitrary", "arbitrary")),
        name="ada",
    )(cond8, w_ada, b_ada.reshape(DEPTH, 1, n))


NORM_ROWS = 8


def _norm_rows(src_ref, rstd_ref, n_rows, emit):
    def tile(r):
        return pl.ds(pl.multiple_of(r * NORM_ROWS, NORM_ROWS), NORM_ROWS)

    def stats(r, carry):
        x = src_ref[tile(r), :]
        rstd_ref[tile(r), :] = lax.rsqrt(jnp.mean(x * x, axis=-1, keepdims=True) + EPS)
        return carry

    def apply(r, carry):
        emit(tile(r), src_ref[tile(r), :] * rstd_ref[tile(r), :])
        return carry

    lax.fori_loop(0, n_rows // NORM_ROWS, stats, 0, unroll=8)
    lax.fori_loop(0, n_rows // NORM_ROWS, apply, 0, unroll=8)


def _prenorm_kernel(x_ref, g_ref, sh_ref, sc_ref, h_ref, rstd_ref, *, tm):
    gs = jnp.broadcast_to(g_ref[...] * (1.0 + sc_ref[...]), (NORM_ROWS, D_MODEL))
    sh = jnp.broadcast_to(sh_ref[...], (NORM_ROWS, D_MODEL))

    def emit(rs, y):
        h_ref[rs, :] = (y * gs + sh).astype(BF16)

    _norm_rows(x_ref, rstd_ref, tm, emit)


def _prenorm(x, norm_g, shift, scale, tokens_per_cond, tm=256):
    m = x.shape[0]
    tpc = tokens_per_cond // tm
    cond = pl.BlockSpec((None, 1, D_MODEL), lambda i: (i // tpc, 0, 0))
    return pl.pallas_call(
        functools.partial(_prenorm_kernel, tm=tm),
        grid=(m // tm,),
        in_specs=[pl.BlockSpec((tm, D_MODEL), lambda i: (i, 0)), pl.BlockSpec((1, D_MODEL), lambda i: (0, 0)),
                  cond, cond],
        out_specs=pl.BlockSpec((tm, D_MODEL), lambda i: (i, 0)),
        out_shape=jax.ShapeDtypeStruct((m, D_MODEL), BF16),
        scratch_shapes=[pltpu.VMEM((tm, 1), F32)],
        compiler_params=_params(("arbitrary",)),
        name="prenorm",
    )(x, norm_g.reshape(1, D_MODEL), shift, scale)


def _inproj_kernel(h_ref, w_ref, o_ref, *, tn):
    acc = _dot(h_ref[...], w_ref[...])
    for s in range(tn // LANE):
        o_ref[s] = acc[:, s * LANE:(s + 1) * LANE]


def _inproj(h, w_bf, layer, tm, tn=512):
    m = h.shape[0]
    n_tiles = IN_W // tn
    kv0 = (4 * W_A + W_B) // tn
    n_kv = 2 * KV_B * HEAD_DIM // tn

    def src(j):
        return jnp.where(j < kv0, j, jnp.where(j < n_tiles - n_kv, j + n_kv, j - (n_tiles - n_kv) + kv0))

    return pl.pallas_call(
        functools.partial(_inproj_kernel, tn=tn),
        grid=(m // tm, n_tiles),
        in_specs=[
            pl.BlockSpec((tm, D_MODEL), lambda i, j: (i, 0)),
            pl.BlockSpec((None, D_MODEL, tn), lambda i, j: (layer, 0, src(j))),
        ],
        out_specs=pl.BlockSpec((tn // LANE, tm, LANE), lambda i, j: (j, i, 0)),
        out_shape=jax.ShapeDtypeStruct((N_CHUNK, m, LANE), F32),
        compiler_params=_params(("arbitrary", "arbitrary")),
        name="inproj",
    )(h, w_bf)


def _ctx_attn_kernel(sink_ref, qa, ka, va, za, qb, zb, kb, vb, ya, yb, nak, nav, nbk, nbv):
    for h in range(H_A):
        cols = slice(h * HEAD_DIM, (h + 1) * HEAD_DIM)
        k32 = ka[h]
        v32 = va[h]
        s = _dot_nt((qa[h] * QK_SCALE_LOG2).astype(BF16), k32.astype(BF16))
        p = jnp.exp2(s - jnp.max(s, axis=-1, keepdims=True))
        o = _dot(p.astype(BF16), v32.astype(BF16)) / jnp.sum(p, axis=-1, keepdims=True)
        ya[:, cols] = (o * _silu(za[h])).astype(BF16)
        nak[:, cols] = k32
        nav[:, cols] = v32
    for kv in range(KV_B):
        cols = slice(kv * HEAD_DIM, (kv + 1) * HEAD_DIM)
        nbk[:, cols] = kb[kv]
        nbv[:, cols] = vb[kv]
    for h in range(H_B):
        cols = slice(h * HEAD_DIM, (h + 1) * HEAD_DIM)
        kv = h // G_B
        sink = sink_ref[h] * LOG2E
        s = _dot_nt((qb[h] * QK_SCALE_LOG2).astype(BF16), kb[kv].astype(BF16))
        m = jnp.maximum(jnp.max(s, axis=-1, keepdims=True), sink)
        p = jnp.exp2(s - m)
        l = jnp.sum(p, axis=-1, keepdims=True) + jnp.exp2(sink - m)
        o = _dot(p.astype(BF16), vb[kv].astype(BF16)) / l
        yb[:, cols] = (o * _silu(zb[h])).astype(BF16)


def _ctx_attn(proj, sink, nb, t):
    m = nb * t

    def grp(g):
        return pl.BlockSpec((CPG, t, LANE), lambda b: (g, b, 0))

    def kvs(c):
        return pl.BlockSpec((KV_B, t, LANE), lambda b: (c // KV_B, b, 0))

    wide = pl.BlockSpec((t, W_A), lambda b: (b, 0))
    wide3 = pl.BlockSpec((None, t, W_A), lambda b: (b, 0, 0))
    narrow3 = pl.BlockSpec((None, t, KV_B * HEAD_DIM), lambda b: (b, 0, 0))
    return pl.pallas_call(
        _ctx_attn_kernel,
        grid=(nb,),
        in_specs=[pl.BlockSpec(memory_space=pltpu.SMEM),
                  grp(G_QA), grp(G_KA), grp(G_VA), grp(G_ZA), grp(G_QB), grp(G_ZB), kvs(C_KB), kvs(C_VB)],
        out_specs=[wide, wide, wide3, wide3, narrow3, narrow3],
        out_shape=[jax.ShapeDtypeStruct((m, W_A), BF16), jax.ShapeDtypeStruct((m, W_B), BF16),
                   jax.ShapeDtypeStruct((nb, t, W_A), F32), jax.ShapeDtypeStruct((nb, t, W_A), F32),
                   jax.ShapeDtypeStruct((nb, t, KV_B * HEAD_DIM), F32),
                   jax.ShapeDtypeStruct((nb, t, KV_B * HEAD_DIM), F32)],
        compiler_params=_params(("arbitrary",)),
        name="ctx_attn",
    )(sink, proj, proj, proj, proj, proj, proj, proj, proj)


def _na_tables(rpb):
    rows = GRID_W
    n_dc = 2 * NA_COLS - 1
    c = np.arange(GRID_W)[:, None]
    kc = np.arange(GRID_W)[None, :]
    start = np.clip(c - NA_COLS // 2, 0, GRID_W - NA_COLS)
    col_ok = (kc >= start) & (kc < start + NA_COLS)
    dc = np.clip(kc - c + NA_COLS - 1, 0, n_dc - 1)
    onehot = (dc[:, None, :] == np.arange(n_dc)[None, :, None]).astype(np.float32)
    toep = jnp.einsum("hrd,cdk->hcrk", rpb * LOG2E, jnp.asarray(onehot), precision=lax.Precision.HIGHEST)
    padded = jnp.concatenate([toep[:, :, :1]] * NA_KR + [toep] + [toep[:, :, -1:]] * NA_KR, axis=2)
    i = np.arange(NA_QR)[:, None]
    j = np.arange(NA_KR)[None, :]
    masks, slabs = [], []
    for r0, ks in ((0, 0), (NA_QR, 0), (rows - NA_QR, rows - NA_KR)):
        r = r0 + i
        rs = np.clip(r - NA_ROWS // 2, 0, rows - NA_ROWS)
        kr = ks + j
        row_ok = (kr >= rs) & (kr < rs + NA_ROWS)
        masks.append(row_ok[:, None, :, None] & col_ok[None, :, None, :])
        for qi in range(NA_QR):
            dr0 = ks - (r0 + qi) + NA_ROWS - 1 + NA_KR
            slabs.append(padded[:, :, dr0:dr0 + NA_KR])
    flat = (NA_QR * GRID_W, NA_KR * GRID_W)
    mask = np.stack([mk.reshape(flat) for mk in masks]).astype(np.float32)
    bias = jnp.stack(slabs, axis=1).reshape((H_A, 3) + flat)
    return bias, jnp.asarray(mask)


def _na_kernel(q_ref, k_ref, v_ref, z_ref, ck_ref, cv_ref, bias_ref, mask_ref, o_ref, *, n_blocks):
    qn = NA_QR * GRID_W
    kn = NA_KR * GRID_W
    ck = ck_ref[...].astype(BF16)
    cv = cv_ref[...].astype(BF16)

    def body(rb, carry):
        q0 = pl.multiple_of(rb * qn, qn)
        ks = jnp.clip(rb * NA_QR - NA_ROWS // 2, 0, n_blocks * NA_QR - NA_KR)
        k0 = pl.multiple_of(ks * GRID_W, GRID_W)
        ty = jnp.where(rb == 0, 0, jnp.where(rb == n_blocks - 1, 2, 1))
        q = (q_ref[pl.ds(q0, qn), :] * QK_SCALE_LOG2).astype(BF16)
        k = k_ref[pl.ds(k0, kn), :].astype(BF16)
        v = v_ref[pl.ds(k0, kn), :].astype(BF16)
        s_loc = _dot_nt(q, k) + bias_ref[ty]
        s_loc = jnp.where(mask_ref[ty] > 0.5, s_loc, NEG)
        s_ctx = _dot_nt(q, ck)
        m = jnp.maximum(jnp.max(s_loc, axis=-1, keepdims=True), jnp.max(s_ctx, axis=-1, keepdims=True))
        p_loc = jnp.exp2(s_loc - m)
        p_ctx = jnp.exp2(s_ctx - m)
        l = jnp.sum(p_loc, axis=-1, keepdims=True) + jnp.sum(p_ctx, axis=-1, keepdims=True)
        o = (_dot(p_loc.astype(BF16), v) + _dot(p_ctx.astype(BF16), cv)) / l
        o_ref[pl.ds(q0, qn), :] = (o * _silu(z_ref[pl.ds(q0, qn), :])).astype(BF16)
        return carry

    lax.fori_loop(0, n_blocks, body, 0, unroll=2)


def _na_attn(proj, cache_k, cache_v, bias, mask, layer, nb, t):
    p_len = cache_k.shape[2]
    n_blocks = t // (NA_QR * GRID_W)

    def head(g):
        return pl.BlockSpec((None, t, LANE), lambda b, h: (g * CPG + h, b, 0))

    ctx = pl.BlockSpec((None, None, p_len, HEAD_DIM), lambda b, h: (b, layer, 0, h))
    return pl.pallas_call(
        functools.partial(_na_kernel, n_blocks=n_blocks),
        grid=(nb, H_A),
        in_specs=[head(G_QA), head(G_KA), head(G_VA), head(G_ZA), ctx, ctx,
                  pl.BlockSpec((None,) + bias.shape[1:], lambda b, h: (h, 0, 0, 0)),
                  pl.BlockSpec(mask.shape, lambda b, h: (0, 0, 0))],
        out_specs=pl.BlockSpec((t, HEAD_DIM), lambda b, h: (b, h)),
        out_shape=jax.ShapeDtypeStruct((nb * t, W_A), BF16),
        compiler_params=_params(("arbitrary", "arbitrary")),
        name="na_attn",
    )(proj, proj, proj, proj, cache_k, cache_v, bias, mask)


def _rope_tables(t):
    pos = np.arange(t)
    half = HEAD_DIM // 2
    inv = 1.0 / (ROPE_BASE ** (np.arange(0, half, 2, dtype=np.float64) / half))
    ang_r = (pos // GRID_W)[:, None] * inv[None, :]
    ang_c = (pos % GRID_W)[:, None] * inv[None, :]
    zero = np.zeros_like(ang_r)
    cos = np.concatenate([np.cos(ang_r), np.cos(ang_r), np.cos(ang_c), np.cos(ang_c)], axis=1)
    s_up = np.concatenate([-np.sin(ang_r), zero, -np.sin(ang_c), zero], axis=1)
    s_dn = np.concatenate([zero, np.sin(ang_r), zero, np.sin(ang_c)], axis=1)
    return tuple(jnp.asarray(a.astype(np.float32)) for a in (cos, s_up, s_dn))


def _band_mask(t):
    row = np.tile(np.arange(BAND), G_B)[:, None]
    col = np.arange(3 * BAND)[None, :]
    out = []
    for off in (0, -BAND, -2 * BAND):
        out.append(np.abs(off + col - row) <= BAND)
    return jnp.asarray(np.stack(out).astype(np.float32))


def _rope(x, cos, s_up, s_dn):
    quarter = HEAD_DIM // 4
    return x * cos + pltpu.roll(x, HEAD_DIM - quarter, 1) * s_up + pltpu.roll(x, quarter, 1) * s_dn


def _band_kernel(sink_ref, q_ref, k_ref, v_ref, z_ref, ck_ref, cv_ref, cos_ref, up_ref, dn_ref, mask_ref,
                 o_ref, kr_ref, vr_ref, *, n_blocks, blocks_per_step):
    kv = pl.program_id(1)
    step = pl.program_id(2)
    t = n_blocks * BAND
    chunk = 512

    @pl.when(step == 0)
    def _():
        def prep(i, carry):
            r0 = pl.multiple_of(i * chunk, chunk)
            sl = pl.ds(r0, chunk)
            kr_ref[sl, :] = _rope(k_ref[sl, :], cos_ref[sl, :], up_ref[sl, :], dn_ref[sl, :]).astype(BF16)
            vr_ref[sl, :] = v_ref[sl, :].astype(BF16)
            return carry

        lax.fori_loop(0, t // chunk, prep, 0)

    ck = ck_ref[...].astype(BF16)
    cv = cv_ref[...].astype(BF16)
    sink = jnp.concatenate([jnp.full((BAND, 1), sink_ref[kv * G_B + g] * LOG2E, F32) for g in range(G_B)], axis=0)

    def body(j, carry):
        i = step * blocks_per_step + j
        ql = pl.ds(pl.multiple_of(j * BAND, BAND), BAND)
        qs = pl.ds(pl.multiple_of(i * BAND, BAND), BAND)
        kb = jnp.clip(i - 1, 0, n_blocks - 3)
        k0 = pl.multiple_of(kb * BAND, BAND)
        ty = jnp.where(i == 0, 0, jnp.where(i == n_blocks - 1, 2, 1))
        cos = cos_ref[qs, :]
        up = up_ref[qs, :]
        dn = dn_ref[qs, :]
        q = jnp.concatenate([(_rope(q_ref[g, ql, :], cos, up, dn) * QK_SCALE_LOG2).astype(BF16)
                             for g in range(G_B)], axis=0)
        k = kr_ref[pl.ds(k0, 3 * BAND), :]
        v = vr_ref[pl.ds(k0, 3 * BAND), :]
        s_loc = jnp.where(mask_ref[ty] > 0.5, _dot_nt(q, k), NEG)
        s_ctx = _dot_nt(q, ck)
        m = jnp.maximum(jnp.maximum(jnp.max(s_loc, axis=-1, keepdims=True),
                                    jnp.max(s_ctx, axis=-1, keepdims=True)), sink)
        p_loc = jnp.exp2(s_loc - m)
        p_ctx = jnp.exp2(s_ctx - m)
        l = jnp.sum(p_loc, axis=-1, keepdims=True) + jnp.sum(p_ctx, axis=-1, keepdims=True) + jnp.exp2(sink - m)
        o = (_dot(p_loc.astype(BF16), v) + _dot(p_ctx.astype(BF16), cv)) / l
        for g in range(G_B):
            og = o[g * BAND:(g + 1) * BAND, :] * _silu(z_ref[g, ql, :])
            o_ref[ql, g * HEAD_DIM:(g + 1) * HEAD_DIM] = og.astype(BF16)
        return carry

    lax.fori_loop(0, blocks_per_step, body, 0, unroll=2)


def _band_attn(proj, sink, cache_k, cache_v, rope, mask, layer, nb, t, tq=1024):
    p_len = cache_k.shape[2]
    n_blocks = t // BAND
    nq = t // tq
    cos, s_up, s_dn = rope
    heads = pl.BlockSpec((G_B, tq, LANE), lambda b, kv, s: (G_QB * CPG // G_B + kv, b * nq + s, 0))
    gates = pl.BlockSpec((G_B, tq, LANE), lambda b, kv, s: (G_ZB * CPG // G_B + kv, b * nq + s, 0))
    keys = pl.BlockSpec((None, t, LANE), lambda b, kv, s: (C_KB + kv, b, 0))
    vals = pl.BlockSpec((None, t, LANE), lambda b, kv, s: (C_VB + kv, b, 0))
    ctx = pl.BlockSpec((None, None, p_len, HEAD_DIM), lambda b, kv, s: (b, layer, 0, kv))
    tab = pl.BlockSpec((t, HEAD_DIM), lambda b, kv, s: (0, 0))
    return pl.pallas_call(
        functools.partial(_band_kernel, n_blocks=n_blocks, blocks_per_step=tq // BAND),
        grid=(nb, KV_B, nq),
        in_specs=[pl.BlockSpec(memory_space=pltpu.SMEM), heads, keys, vals, gates, ctx, ctx, tab, tab, tab,
                  pl.BlockSpec(mask.shape, lambda b, kv, s: (0, 0, 0))],
        out_specs=pl.BlockSpec((tq, G_B * HEAD_DIM), lambda b, kv, s: (b * nq + s, kv)),
        out_shape=jax.ShapeDtypeStruct((nb * t, W_B), BF16),
        scratch_shapes=[pltpu.VMEM((t, HEAD_DIM), BF16), pltpu.VMEM((t, HEAD_DIM), BF16)],
        compiler_params=_params(("arbitrary", "arbitrary", "arbitrary")),
        name="band_attn",
    )(sink, proj, proj, proj, proj, cache_k, cache_v, cos, s_up, s_dn, mask)


def _cd_kernel(uc, ucp, ucn, zc, pa, pap, pan, pb, pbp, pbn, zd, wpool, pscale, cw, cb, lng, lnb, wpw, bpw,
               o_ref, ue, he, cv, hn, *, tt, t_seq):
    i = pl.program_id(1)
    has_prev = i > 0
    has_next = i < pl.num_programs(1) - 1
    ext = tt + 2 * HALO

    for c in range(CPG):
        ue[c, 0:HALO, :] = jnp.where(has_prev, ucp[c], 0.0)
        ue[c, HALO:HALO + tt, :] = uc[c]
        ue[c, HALO + tt:ext, :] = jnp.where(has_next, ucn[c], 0.0)
        he[c, 0:HALO, :] = jnp.where(has_prev, pap[c] * jax.nn.sigmoid(pbp[c]), 0.0)
        he[c, HALO:HALO + tt, :] = pa[c] * jax.nn.sigmoid(pb[c])
        he[c, HALO + tt:ext, :] = jnp.where(has_next, pan[c] * jax.nn.sigmoid(pbn[c]), 0.0)

    rt = 64
    for r0 in range(0, tt, rt):
        edge_tile = r0 == 0 or r0 + rt == tt
        tpos = i * tt + r0 + lax.broadcasted_iota(jnp.int32, (rt, LANE), 0)
        for g, w in enumerate(POOL_WINDOWS):
            half = w // 2
            if edge_tile:
                cnt = (jnp.minimum(tpos + half, t_seq) - jnp.maximum(tpos - half, 0)).astype(F32)
            d = []
            for c in (2 * g, 2 * g + 1):
                acc = ue[c, HALO + r0 - half:HALO + r0 - half + rt, :]
                for o in range(-half + 1, half):
                    acc = acc + ue[c, HALO + r0 + o:HALO + r0 + o + rt, :]
                mean = acc / cnt if edge_tile else acc * (1.0 / w)
                d.append(mean - uc[c, r0:r0 + rt, :])
            dg = jnp.concatenate(d, axis=1).astype(BF16)
            cols = slice(g * POOL_GW, (g + 1) * POOL_GW)
            y = _dot(dg, wpool[g]) * pscale[:, cols]
            z = jnp.concatenate([zc[2 * g, r0:r0 + rt, :], zc[2 * g + 1, r0:r0 + rt, :]], axis=1)
            o_ref[r0:r0 + rt, cols] = (y * _silu(z)).astype(BF16)

    def conv_chunk(c, carry):
        wts = cw[c]
        bias = cb[c]
        for r0 in range(0, tt, rt):
            acc = jnp.zeros((rt, LANE), F32) + bias
            for k in range(CONV_K):
                off = HALO - CONV_K // 2 + k + r0
                acc = acc + he[c, off:off + rt, :] * wts[k:k + 1, :]
            cv[c, r0:r0 + rt, :] = acc
        return carry

    lax.fori_loop(0, CPG, conv_chunk, 0)

    def ln_rows(r, carry):
        r0 = pl.multiple_of(r * rt, rt)
        rs = pl.ds(r0, rt)
        xs = [cv[c, rs, :] for c in range(CPG)]
        tot = xs[0]
        for x in xs[1:]:
            tot = tot + x
        mu = jnp.sum(tot, axis=-1, keepdims=True) * (1.0 / W_D)
        sq = jnp.square(xs[0] - mu)
        for x in xs[1:]:
            sq = sq + jnp.square(x - mu)
        rstd = lax.rsqrt(jnp.sum(sq, axis=-1, keepdims=True) * (1.0 / W_D) + EPS)
        for c in range(CPG):
            cols = slice(c * LANE, (c + 1) * LANE)
            y = (xs[c] - mu) * rstd * lng[:, cols] + lnb[:, cols]
            hn[rs, cols] = _silu(y).astype(BF16)
        return carry

    lax.fori_loop(0, tt // rt, ln_rows, 0, unroll=2)

    for r0 in range(0, tt, 256):
        rs = slice(r0, r0 + 256)
        y = _dot(hn[rs, :], wpw[...]) + bpw[...]
        for c in range(CPG):
            cols = slice(c * LANE, (c + 1) * LANE)
            o_ref[rs, W_C + c * LANE:W_C + (c + 1) * LANE] = (y[:, cols] * _silu(zd[c, rs, :])).astype(BF16)


def _cd_mixers(proj, w_pool_bf, pool_scale, conv_w, conv_b, ln_g, ln_b, w_pw2_bf, b_pw2, nb, t, tt):
    m = nb * t
    nt = t // tt
    hb = tt // HALO
    last = m // HALO - 1

    def cur(g):
        return pl.BlockSpec((CPG, tt, LANE), lambda b, i: (g, b * nt + i, 0))

    def prev(g):
        return pl.BlockSpec((CPG, HALO, LANE), lambda b, i: (g, jnp.maximum((b * nt + i) * hb - 1, 0), 0))

    def nxt(g):
        return pl.BlockSpec((CPG, HALO, LANE), lambda b, i: (g, jnp.minimum((b * nt + i + 1) * hb, last), 0))

    def full(shape):
        return pl.BlockSpec(shape, lambda b, i: (0,) * len(shape))

    ext = tt + 2 * HALO
    cw = conv_w.reshape(CONV_K, CPG, LANE).transpose(1, 0, 2)
    cb = conv_b.reshape(CPG, 1, LANE)
    return pl.pallas_call(
        functools.partial(_cd_kernel, tt=tt, t_seq=t),
        grid=(nb, nt),
        in_specs=[cur(G_UC), prev(G_UC), nxt(G_UC), cur(G_ZC),
                  cur(G_PA), prev(G_PA), nxt(G_PA), cur(G_PB), prev(G_PB), nxt(G_PB), cur(G_ZD),
                  full((N_POOL, POOL_GW, POOL_GW)), full((1, W_C)), full((CPG, CONV_K, LANE)),
                  full((CPG, 1, LANE)), full((1, W_D)), full((1, W_D)), full((W_D, W_D)), full((1, W_D))],
        out_specs=pl.BlockSpec((tt, W_C + W_D), lambda b, i: (b * nt + i, 0)),
        out_shape=jax.ShapeDtypeStruct((m, W_C + W_D), BF16),
        scratch_shapes=[pltpu.VMEM((CPG, ext, LANE), F32), pltpu.VMEM((CPG, ext, LANE), F32),
                        pltpu.VMEM((CPG, tt, LANE), F32), pltpu.VMEM((tt, W_D), BF16)],
        compiler_params=_params(("arbitrary", "arbitrary")),
        name="cd_mixers",
    )(proj, proj, proj, proj, proj, proj, proj, proj, proj, proj, proj,
      w_pool_bf, pool_scale.reshape(1, W_C), cw, cb, ln_g.reshape(1, W_D), ln_b.reshape(1, W_D),
      w_pw2_bf, b_pw2.reshape(1, W_D))


def _outproj_kernel(*refs, tn, tm, last):
    if last:
        ya, yb, ycd, w_ref, x_ref, gate_ref, g_ref, y_ref, stage, rstd_ref = refs
    else:
        ya, yb, ycd, w_ref, x_ref, gate_ref, g_ref, sh_ref, sc_ref, xo_ref, h_ref, stage, rstd_ref = refs
    j = pl.program_id(1)
    acc = _dot(ya[...], w_ref[0:W_A, :])
    acc = acc + _dot(yb[...], w_ref[W_A:W_A + W_B, :])
    acc = acc + _dot(ycd[...], w_ref[W_A + W_B:, :])
    x_new = x_ref[...] + gate_ref[...] * acc
    stage[:, pl.ds(pl.multiple_of(j * tn, tn), tn)] = x_new
    if not last:
        xo_ref[...] = x_new

    @pl.when(j == pl.num_programs(1) - 1)
    def _():
        if last:
            g = jnp.broadcast_to(g_ref[...], (NORM_ROWS, D_MODEL))

            def emit(rs, y):
                y_ref[rs, :] = y * g
        else:
            gs = jnp.broadcast_to(g_ref[...] * (1.0 + sc_ref[...]), (NORM_ROWS, D_MODEL))
            sh = jnp.broadcast_to(sh_ref[...], (NORM_ROWS, D_MODEL))

            def emit(rs, y):
                h_ref[rs, :] = (y * gs + sh).astype(BF16)

        _norm_rows(stage, rstd_ref, tm, emit)


def _outproj(ya, yb, ycd, w_bf, x, gate, norm, layer, tokens_per_cond, tm=512, tn=1024):
    m = x.shape[0]
    tpc = tokens_per_cond // tm
    last = len(norm) == 1
    rows = pl.BlockSpec((tm, D_MODEL), lambda i, j: (i, 0))
    cond = pl.BlockSpec((None, 1, D_MODEL), lambda i, j: (i // tpc, 0, 0))
    in_specs = [
        pl.BlockSpec((tm, W_A), lambda i, j: (i, 0)),
        pl.BlockSpec((tm, W_B), lambda i, j: (i, 0)),
        pl.BlockSpec((tm, W_C + W_D), lambda i, j: (i, 0)),
        pl.BlockSpec((None, D_MODEL, tn), lambda i, j: (layer, 0, j)),
        pl.BlockSpec((tm, tn), lambda i, j: (i, j)),
        pl.BlockSpec((None, 1, tn), lambda i, j: (i // tpc, 0, j)),
        pl.BlockSpec((1, D_MODEL), lambda i, j: (0, 0)),
    ]
    args = [ya, yb, ycd, w_bf, x, gate, norm[0].reshape(1, D_MODEL)]
    scratch = [pltpu.VMEM((tm, D_MODEL), F32), pltpu.VMEM((tm, 1), F32)]
    if last:
        out_specs = rows
        out_shape = jax.ShapeDtypeStruct((m, D_MODEL), F32)
    else:
        in_specs += [cond, cond]
        args += [norm[1], norm[2]]
        out_specs = [pl.BlockSpec((tm, tn), lambda i, j: (i, j)), rows]
        out_shape = [jax.ShapeDtypeStruct((m, D_MODEL), F32), jax.ShapeDtypeStruct((m, D_MODEL), BF16)]
    return pl.pallas_call(
        functools.partial(_outproj_kernel, tn=tn, tm=tm, last=last),
        grid=(m // tm, D_MODEL // tn),
        in_specs=in_specs,
        out_specs=out_specs,
        out_shape=out_shape,
        scratch_shapes=scratch,
        compiler_params=_params(("arbitrary", "arbitrary")),
        name="outproj",
    )(*args)


def kernel(x_prompt, x_sample, cache_a_k, cache_a_v, cache_b_k, cache_b_v, c, c_ctx, norm_g, w_ada, b_ada, w_in,
           rpb_a, sink_b, w_pool, pool_scale, conv_w, conv_b, ln_g, ln_b, w_pw2, b_pw2, w_out, final_g):
    nbp, tp, _ = x_prompt.shape
    nbs, ts, _ = x_sample.shape
    p_len = cache_a_k.shape[2]
    assert ts == GRID_W * GRID_W and nbs + 1 <= 8

    w_in_bf = w_in.astype(BF16)
    w_out_bf = w_out.astype(BF16)
    w_pool_bf = w_pool.astype(BF16)
    w_pw2_bf = w_pw2.astype(BF16)
    cak = cache_a_k.reshape(nbs, DEPTH, p_len, W_A)
    cav = cache_a_v.reshape(nbs, DEPTH, p_len, W_A)
    cbk = cache_b_k.reshape(nbs, DEPTH, p_len, KV_B * HEAD_DIM)
    cbv = cache_b_v.reshape(nbs, DEPTH, p_len, KV_B * HEAD_DIM)

    cond8 = jnp.concatenate([c_ctx[None, :], c, jnp.zeros((8 - 1 - nbs, D_MODEL), F32)], axis=0)
    ada = _ada(cond8, w_ada, b_ada)
    rope = _rope_tables(ts)
    band_mask = _band_mask(ts)

    def mod(l, k, lo, hi):
        return ada[l, lo:hi, k * D_MODEL:(k + 1) * D_MODEL].reshape(hi - lo, 1, D_MODEL)

    def next_norm(l, lo, hi):
        if l == DEPTH - 1:
            return (final_g,)
        return (norm_g[l + 1], mod(l + 1, 0, lo, hi), mod(l + 1, 1, lo, hi))

    xp = x_prompt.reshape(nbp * tp, D_MODEL)
    xs = x_sample.reshape(nbs * ts, D_MODEL)
    hp = _prenorm(xp, norm_g[0], mod(0, 0, 0, 1), mod(0, 1, 0, 1), nbp * tp)
    hs = _prenorm(xs, norm_g[0], mod(0, 0, 1, 1 + nbs), mod(0, 1, 1, 1 + nbs), ts)
    kv_out = [[], [], [], []]
    for l in range(DEPTH):
        na_bias, na_mask = _na_tables(rpb_a[l])

        proj = _inproj(hp, w_in_bf, l, tm=2048)
        ya, yb, nak, nav, nbk, nbv = _ctx_attn(proj, sink_b[l], nbp, tp)
        ycd = _cd_mixers(proj, w_pool_bf[l], pool_scale[l], conv_w[l], conv_b[l], ln_g[l], ln_b[l],
                         w_pw2_bf[l], b_pw2[l], nbp, tp, tt=tp)
        res = _outproj(ya, yb, ycd, w_out_bf, xp, mod(l, 2, 0, 1), next_norm(l, 0, 1), l, nbp * tp)
        if l == DEPTH - 1:
            y_prompt = res.reshape(nbp, tp, D_MODEL)
        else:
            xp, hp = res
        for lst, a in zip(kv_out, (nak, nav, nbk, nbv)):
            lst.append(a)

        proj = _inproj(hs, w_in_bf, l, tm=2048)
        ya = _na_attn(proj, cak, cav, na_bias, na_mask, l, nbs, ts)
        yb = _band_attn(proj, sink_b[l], cbk, cbv, rope, band_mask, l, nbs, ts)
        ycd = _cd_mixers(proj, w_pool_bf[l], pool_scale[l], conv_w[l], conv_b[l], ln_g[l], ln_b[l],
                         w_pw2_bf[l], b_pw2[l], nbs, ts, tt=512)
        res = _outproj(ya, yb, ycd, w_out_bf, xs, mod(l, 2, 1, 1 + nbs), next_norm(l, 1, 1 + nbs), l, ts)
        if l == DEPTH - 1:
            y_sample = res.reshape(nbs, ts, D_MODEL)
        else:
            xs, hs = res

    new_a_k = jnp.stack(kv_out[0], axis=1).reshape(nbp, DEPTH, tp, H_A, HEAD_DIM)
    new_a_v = jnp.stack(kv_out[1], axis=1).reshape(nbp, DEPTH, tp, H_A, HEAD_DIM)
    new_b_k = jnp.stack(kv_out[2], axis=1).reshape(nbp, DEPTH, tp, KV_B, HEAD_DIM)
    new_b_v = jnp.stack(kv_out[3], axis=1).reshape(nbp, DEPTH, tp, KV_B, HEAD_DIM)
    return (y_prompt, y_sample, new_a_k, new_a_v, new_b_k, new_b_v)
```

```python
import functools

import numpy as np
import jax
import jax.numpy as jnp
from jax import lax
from jax.experimental import pallas as pl
from jax.experimental.pallas import tpu as pltpu

D_MODEL = 4096
DEPTH = 2
GRID_W = 64
HEAD_DIM = 128
W_A = D_MODEL // 4
W_B = D_MODEL // 4
W_C = D_MODEL // 4
W_D = D_MODEL - W_A - W_B - W_C
H_A = W_A // HEAD_DIM
H_B = W_B // HEAD_DIM
KV_B = max(1, H_B // 4)
G_B = H_B // KV_B
NA_ROWS = 8
NA_COLS = 16
BAND = 128
N_POOL = 4
POOL_WINDOWS = (2, 4, 8, 16)
POOL_GW = W_C // N_POOL
CONV_K = 31
ROPE_BASE = 10000.0
EPS = 1e-6
NEG = -1e30
LOG2E = float(np.log2(np.e))
QK_SCALE_LOG2 = HEAD_DIM ** -0.5 * LOG2E
IN_W = 4 * W_A + 2 * W_B + 2 * KV_B * HEAD_DIM + 2 * W_C + 3 * W_D

LANE = 128
N_CHUNK = IN_W // LANE
CPG = W_A // LANE
G_QA, G_KA, G_VA, G_ZA, G_QB, G_ZB, G_UC, G_ZC, G_PA, G_PB, G_ZD = range(11)
C_KB = 11 * CPG
C_VB = C_KB + KV_B
HALO = 16
NA_QR = 4
NA_KR = 12
VMEM_LIMIT = 56 * 1024 * 1024
ADA_TN = 512
PRENORM_TM = 256
INPROJ_TM, INPROJ_TN = 2048, 512
OUTPROJ_TM, OUTPROJ_TN = 512, 1024
CD_TT = 512
BAND_TQ = 1024

F32 = jnp.float32
BF16 = jnp.bfloat16


def _silu(x):
    return x * jax.nn.sigmoid(x)


def _dot(a, b):
    return jnp.dot(a, b, preferred_element_type=F32)


def _dot_nt(a, b):
    return lax.dot_general(a, b, (((1,), (1,)), ((), ())), preferred_element_type=F32)


def _params(sem):
    return pltpu.CompilerParams(dimension_semantics=sem, vmem_limit_bytes=VMEM_LIMIT)


def _ada_kernel(c_ref, w_ref, b_ref, o_ref):
    s = _silu(c_ref[...])
    o_ref[...] = _dot(s.astype(BF16), w_ref[...].astype(BF16)) + b_ref[...]


def _ada(cond8, w_ada, b_ada):
    tn = ADA_TN
    n = 3 * D_MODEL
    return pl.pallas_call(
        _ada_kernel,
        grid=(DEPTH, n // tn),
        in_specs=[
            pl.BlockSpec((8, D_MODEL), lambda l, j: (0, 0)),
            pl.BlockSpec((None, D_MODEL, tn), lambda l, j: (l, 0, j)),
            pl.BlockSpec((None, 1, tn), lambda l, j: (l, 0, j)),
        ],
        out_specs=pl.BlockSpec((None, 8, tn), lambda l, j: (l, 0, j)),
        out_shape=jax.ShapeDtypeStruct((DEPTH, 8, n), F32),
        compiler_params=_params(("arbitrary", "arbitrary")),
        name="ada",
    )(cond8, w_ada, b_ada.reshape(DEPTH, 1, n))


NORM_ROWS = 8


def _norm_tile(r):
    return pl.ds(pl.multiple_of(r * NORM_ROWS, NORM_ROWS), NORM_ROWS)


def _sumsq_sweep(src_ref, ssq_ref, n_rows):
    def stats(r, carry):
        x = src_ref[_norm_tile(r), :]
        ssq_ref[_norm_tile(r), :] = jnp.sum(x * x, axis=-1, keepdims=True)
        return carry

    lax.fori_loop(0, n_rows // NORM_ROWS, stats, 0, unroll=8)


def _normalise_sweep(src_ref, ssq_ref, n_rows, emit):
    def apply(r, carry):
        rstd = lax.rsqrt(ssq_ref[_norm_tile(r), :] * (1.0 / D_MODEL) + EPS)
        emit(_norm_tile(r), src_ref[_norm_tile(r), :] * rstd)
        return carry

    lax.fori_loop(0, n_rows // NORM_ROWS, apply, 0, unroll=8)


def _prenorm_kernel(x_ref, g_ref, sh_ref, sc_ref, h_ref, ssq_ref, *, tm):
    gs = jnp.broadcast_to(g_ref[...] * (1.0 + sc_ref[...]), (NORM_ROWS, D_MODEL))
    sh = jnp.broadcast_to(sh_ref[...], (NORM_ROWS, D_MODEL))

    def emit(rs, y):
        h_ref[rs, :] = (y * gs + sh).astype(BF16)

    _sumsq_sweep(x_ref, ssq_ref, tm)
    _normalise_sweep(x_ref, ssq_ref, tm, emit)


def _prenorm(x, norm_g, shift, scale, tokens_per_cond, tm=PRENORM_TM):
    m = x.shape[0]
    tpc = tokens_per_cond // tm
    cond = pl.BlockSpec((None, 1, D_MODEL), lambda i: (i // tpc, 0, 0))
    return pl.pallas_call(
        functools.partial(_prenorm_kernel, tm=tm),
        grid=(m // tm,),
        in_specs=[pl.BlockSpec((tm, D_MODEL), lambda i: (i, 0)), pl.BlockSpec((1, D_MODEL), lambda i: (0, 0)),
                  cond, cond],
        out_specs=pl.BlockSpec((tm, D_MODEL), lambda i: (i, 0)),
        out_shape=jax.ShapeDtypeStruct((m, D_MODEL), BF16),
        scratch_shapes=[pltpu.VMEM((tm, 1), F32)],
        compiler_params=_params(("arbitrary",)),
        name="prenorm",
    )(x, norm_g.reshape(1, D_MODEL), shift, scale)


def _inproj_kernel(h_ref, w_ref, o_ref, *, tn):
    acc = _dot(h_ref[...], w_ref[...])
    for s in range(tn // LANE):
        o_ref[s] = acc[:, s * LANE:(s + 1) * LANE]


def _inproj(h, w_bf, layer, tm=INPROJ_TM, tn=INPROJ_TN):
    m = h.shape[0]
    n_tiles = IN_W // tn
    kv0 = (4 * W_A + W_B) // tn
    n_kv = 2 * KV_B * HEAD_DIM // tn

    def src(j):
        return jnp.where(j < kv0, j, jnp.where(j < n_tiles - n_kv, j + n_kv, j - (n_tiles - n_kv) + kv0))

    return pl.pallas_call(
        functools.partial(_inproj_kernel, tn=tn),
        grid=(m // tm, n_tiles),
        in_specs=[
            pl.BlockSpec((tm, D_MODEL), lambda i, j: (i, 0)),
            pl.BlockSpec((None, D_MODEL, tn), lambda i, j: (layer, 0, src(j))),
        ],
        out_specs=pl.BlockSpec((tn // LANE, tm, LANE), lambda i, j: (j, i, 0)),
        out_shape=jax.ShapeDtypeStruct((N_CHUNK, m, LANE), F32),
        compiler_params=_params(("arbitrary", "arbitrary")),
        name="inproj",
    )(h, w_bf)


def _ctx_attn_kernel(sink_ref, qa, ka, va, za, qb, zb, kb, vb, ya, yb, nak, nav, nbk, nbv):
    for h in range(H_A):
        cols = slice(h * HEAD_DIM, (h + 1) * HEAD_DIM)
        k32 = ka[h]
        v32 = va[h]
        s = _dot_nt((qa[h] * QK_SCALE_LOG2).astype(BF16), k32.astype(BF16))
        p = jnp.exp2(s - jnp.max(s, axis=-1, keepdims=True))
        o = _dot(p.astype(BF16), v32.astype(BF16)) / jnp.sum(p, axis=-1, keepdims=True)
        ya[:, cols] = (o * _silu(za[h])).astype(BF16)
        nak[:, cols] = k32
        nav[:, cols] = v32
    for kv in range(KV_B):
        cols = slice(kv * HEAD_DIM, (kv + 1) * HEAD_DIM)
        nbk[:, cols] = kb[kv]
        nbv[:, cols] = vb[kv]
    for h in range(H_B):
        cols = slice(h * HEAD_DIM, (h + 1) * HEAD_DIM)
        kv = h // G_B
        sink = sink_ref[h] * LOG2E
        s = _dot_nt((qb[h] * QK_SCALE_LOG2).astype(BF16), kb[kv].astype(BF16))
        m = jnp.maximum(jnp.max(s, axis=-1, keepdims=True), sink)
        p = jnp.exp2(s - m)
        l = jnp.sum(p, axis=-1, keepdims=True) + jnp.exp2(sink - m)
        o = _dot(p.astype(BF16), vb[kv].astype(BF16)) / l
        yb[:, cols] = (o * _silu(zb[h])).astype(BF16)


def _ctx_attn(proj, sink, nb, t):
    m = nb * t

    def grp(g):
        return pl.BlockSpec((CPG, t, LANE), lambda b: (g, b, 0))

    def kvs(c):
        return pl.BlockSpec((KV_B, t, LANE), lambda b: (c // KV_B, b, 0))

    wide = pl.BlockSpec((t, W_A), lambda b: (b, 0))
    wide3 = pl.BlockSpec((None, t, W_A), lambda b: (b, 0, 0))
    narrow3 = pl.BlockSpec((None, t, KV_B * HEAD_DIM), lambda b: (b, 0, 0))
    return pl.pallas_call(
        _ctx_attn_kernel,
        grid=(nb,),
        in_specs=[pl.BlockSpec(memory_space=pltpu.SMEM),
                  grp(G_QA), grp(G_KA), grp(G_VA), grp(G_ZA), grp(G_QB), grp(G_ZB), kvs(C_KB), kvs(C_VB)],
        out_specs=[wide, wide, wide3, wide3, narrow3, narrow3],
        out_shape=[jax.ShapeDtypeStruct((m, W_A), BF16), jax.ShapeDtypeStruct((m, W_B), BF16),
                   jax.ShapeDtypeStruct((nb, t, W_A), F32), jax.ShapeDtypeStruct((nb, t, W_A), F32),
                   jax.ShapeDtypeStruct((nb, t, KV_B * HEAD_DIM), F32),
                   jax.ShapeDtypeStruct((nb, t, KV_B * HEAD_DIM), F32)],
        compiler_params=_params(("arbitrary",)),
        name="ctx_attn",
    )(sink, proj, proj, proj, proj, proj, proj, proj, proj)


def _na_tables(rpb):
    rows = GRID_W
    n_dc = 2 * NA_COLS - 1
    c = np.arange(GRID_W)[:, None]
    kc = np.arange(GRID_W)[None, :]
    start = np.clip(c - NA_COLS // 2, 0, GRID_W - NA_COLS)
    col_ok = (kc >= start) & (kc < start + NA_COLS)
    dc = np.clip(kc - c + NA_COLS - 1, 0, n_dc - 1)
    onehot = (dc[:, None, :] == np.arange(n_dc)[None, :, None]).astype(np.float32)
    toep = jnp.einsum("hrd,cdk->hcrk", rpb * LOG2E, jnp.asarray(onehot), precision=lax.Precision.HIGHEST)
    padded = jnp.concatenate([toep[:, :, :1]] * NA_KR + [toep] + [toep[:, :, -1:]] * NA_KR, axis=2)
    i = np.arange(NA_QR)[:, None]
    j = np.arange(NA_KR)[None, :]
    masks, slabs = [], []
    for r0, ks in ((0, 0), (NA_QR, 0), (rows - NA_QR, rows - NA_KR)):
        r = r0 + i
        rs = np.clip(r - NA_ROWS // 2, 0, rows - NA_ROWS)
        kr = ks + j
        row_ok = (kr >= rs) & (kr < rs + NA_ROWS)
        masks.append(row_ok[:, None, :, None] & col_ok[None, :, None, :])
        for qi in range(NA_QR):
            dr0 = ks - (r0 + qi) + NA_ROWS - 1 + NA_KR
            slabs.append(padded[:, :, dr0:dr0 + NA_KR])
    flat = (NA_QR * GRID_W, NA_KR * GRID_W)
    mask = np.stack([mk.reshape(flat) for mk in masks]).astype(np.float32)
    bias = jnp.stack(slabs, axis=1).reshape((H_A, 3) + flat)
    return bias, jnp.asarray(mask)


def _na_kernel(q_ref, k_ref, v_ref, z_ref, ck_ref, cv_ref, bias_ref, mask_ref, o_ref, *, n_blocks):
    qn = NA_QR * GRID_W
    kn = NA_KR * GRID_W
    ck = ck_ref[...].astype(BF16)
    cv = cv_ref[...].astype(BF16)

    def body(rb, carry):
        q0 = pl.multiple_of(rb * qn, qn)
        ks = jnp.clip(rb * NA_QR - NA_ROWS // 2, 0, n_blocks * NA_QR - NA_KR)
        k0 = pl.multiple_of(ks * GRID_W, GRID_W)
        ty = jnp.where(rb == 0, 0, jnp.where(rb == n_blocks - 1, 2, 1))
        q = (q_ref[pl.ds(q0, qn), :] * QK_SCALE_LOG2).astype(BF16)
        k = k_ref[pl.ds(k0, kn), :].astype(BF16)
        v = v_ref[pl.ds(k0, kn), :].astype(BF16)
        s_loc = _dot_nt(q, k) + bias_ref[ty]
        s_loc = jnp.where(mask_ref[ty] > 0.5, s_loc, NEG)
        s_ctx = _dot_nt(q, ck)
        m = jnp.maximum(jnp.max(s_loc, axis=-1, keepdims=True), jnp.max(s_ctx, axis=-1, keepdims=True))
        p_loc = jnp.exp2(s_loc - m)
        p_ctx = jnp.exp2(s_ctx - m)
        l = jnp.sum(p_loc, axis=-1, keepdims=True) + jnp.sum(p_ctx, axis=-1, keepdims=True)
        o = (_dot(p_loc.astype(BF16), v) + _dot(p_ctx.astype(BF16), cv)) / l
        o_ref[pl.ds(q0, qn), :] = (o * _silu(z_ref[pl.ds(q0, qn), :])).astype(BF16)
        return carry

    lax.fori_loop(0, n_blocks, body, 0, unroll=4)


def _na_attn(proj, cache_k, cache_v, bias, mask, layer, nb, t):
    p_len = cache_k.shape[2]
    n_blocks = t // (NA_QR * GRID_W)

    def head(g):
        return pl.BlockSpec((None, t, LANE), lambda b, h: (g * CPG + h, b, 0))

    ctx = pl.BlockSpec((None, None, p_len, HEAD_DIM), lambda b, h: (b, layer, 0, h))
    return pl.pallas_call(
        functools.partial(_na_kernel, n_blocks=n_blocks),
        grid=(nb, H_A),
        in_specs=[head(G_QA), head(G_KA), head(G_VA), head(G_ZA), ctx, ctx,
                  pl.BlockSpec((None,) + bias.shape[1:], lambda b, h: (h, 0, 0, 0)),
                  pl.BlockSpec(mask.shape, lambda b, h: (0, 0, 0))],
        out_specs=pl.BlockSpec((t, HEAD_DIM), lambda b, h: (b, h)),
        out_shape=jax.ShapeDtypeStruct((nb * t, W_A), BF16),
        compiler_params=_params(("arbitrary", "arbitrary")),
        name="na_attn",
    )(proj, proj, proj, proj, cache_k, cache_v, bias, mask)


def _rope_tables(t):
    pos = np.arange(t)
    half = HEAD_DIM // 2
    inv = 1.0 / (ROPE_BASE ** (np.arange(0, half, 2, dtype=np.float64) / half))
    ang_r = (pos // GRID_W)[:, None] * inv[None, :]
    ang_c = (pos % GRID_W)[:, None] * inv[None, :]
    zero = np.zeros_like(ang_r)
    cos = np.concatenate([np.cos(ang_r), np.cos(ang_r), np.cos(ang_c), np.cos(ang_c)], axis=1)
    s_up = np.concatenate([-np.sin(ang_r), zero, -np.sin(ang_c), zero], axis=1)
    s_dn = np.concatenate([zero, np.sin(ang_r), zero, np.sin(ang_c)], axis=1)
    return tuple(jnp.asarray(a.astype(np.float32)) for a in (cos, s_up, s_dn))


def _band_mask(t):
    row = np.tile(np.arange(BAND), G_B)[:, None]
    col = np.arange(3 * BAND)[None, :]
    out = []
    for off in (0, -BAND, -2 * BAND):
        out.append(np.abs(off + col - row) <= BAND)
    return jnp.asarray(np.stack(out).astype(np.float32))


def _rope(x, cos, s_up, s_dn):
    quarter = HEAD_DIM // 4
    return x * cos + pltpu.roll(x, HEAD_DIM - quarter, 1) * s_up + pltpu.roll(x, quarter, 1) * s_dn


def _band_kernel(sink_ref, q_ref, k_ref, v_ref, z_ref, ck_ref, cv_ref, cos_ref, up_ref, dn_ref, mask_ref,
                 o_ref, kr_ref, vr_ref, *, n_blocks, blocks_per_step):
    kv = pl.program_id(1)
    step = pl.program_id(2)
    t = n_blocks * BAND
    chunk = 512

    @pl.when(step == 0)
    def _():
        def prep(i, carry):
            r0 = pl.multiple_of(i * chunk, chunk)
            sl = pl.ds(r0, chunk)
            kr_ref[sl, :] = _rope(k_ref[sl, :], cos_ref[sl, :], up_ref[sl, :], dn_ref[sl, :]).astype(BF16)
            vr_ref[sl, :] = v_ref[sl, :].astype(BF16)
            return carry

        lax.fori_loop(0, t // chunk, prep, 0)

    ck = ck_ref[...].astype(BF16)
    cv = cv_ref[...].astype(BF16)
    sink = jnp.concatenate([jnp.full((BAND, 1), sink_ref[kv * G_B + g] * LOG2E, F32) for g in range(G_B)], axis=0)

    def body(j, carry):
        i = step * blocks_per_step + j
        ql = pl.ds(pl.multiple_of(j * BAND, BAND), BAND)
        qs = pl.ds(pl.multiple_of(i * BAND, BAND), BAND)
        kb = jnp.clip(i - 1, 0, n_blocks - 3)
        k0 = pl.multiple_of(kb * BAND, BAND)
        ty = jnp.where(i == 0, 0, jnp.where(i == n_blocks - 1, 2, 1))
        cos = cos_ref[qs, :]
        up = up_ref[qs, :]
        dn = dn_ref[qs, :]
        q = jnp.concatenate([(_rope(q_ref[g, ql, :], cos, up, dn) * QK_SCALE_LOG2).astype(BF16)
                             for g in range(G_B)], axis=0)
        k = kr_ref[pl.ds(k0, 3 * BAND), :]
        v = vr_ref[pl.ds(k0, 3 * BAND), :]
        s_loc = jnp.where(mask_ref[ty] > 0.5, _dot_nt(q, k), NEG)
        s_ctx = _dot_nt(q, ck)
        m = jnp.maximum(jnp.maximum(jnp.max(s_loc, axis=-1, keepdims=True),
                                    jnp.max(s_ctx, axis=-1, keepdims=True)), sink)
        p_loc = jnp.exp2(s_loc - m)
        p_ctx = jnp.exp2(s_ctx - m)
        l = jnp.sum(p_loc, axis=-1, keepdims=True) + jnp.sum(p_ctx, axis=-1, keepdims=True) + jnp.exp2(sink - m)
        o = (_dot(p_loc.astype(BF16), v) + _dot(p_ctx.astype(BF16), cv)) / l
        for g in range(G_B):
            og = o[g * BAND:(g + 1) * BAND, :] * _silu(z_ref[g, ql, :])
            o_ref[ql, g * HEAD_DIM:(g + 1) * HEAD_DIM] = og.astype(BF16)
        return carry

    lax.fori_loop(0, blocks_per_step, body, 0, unroll=2)


def _band_attn(proj, sink, cache_k, cache_v, rope, mask, layer, nb, t, tq=BAND_TQ):
    p_len = cache_k.shape[2]
    n_blocks = t // BAND
    nq = t // tq
    cos, s_up, s_dn = rope
    heads = pl.BlockSpec((G_B, tq, LANE), lambda b, kv, s: (G_QB * CPG // G_B + kv, b * nq + s, 0))
    gates = pl.BlockSpec((G_B, tq, LANE), lambda b, kv, s: (G_ZB * CPG // G_B + kv, b * nq + s, 0))
    keys = pl.BlockSpec((None, t, LANE), lambda b, kv, s: (C_KB + kv, b, 0))
    vals = pl.BlockSpec((None, t, LANE), lambda b, kv, s: (C_VB + kv, b, 0))
    ctx = pl.BlockSpec((None, None, p_len, HEAD_DIM), lambda b, kv, s: (b, layer, 0, kv))
    tab = pl.BlockSpec((t, HEAD_DIM), lambda b, kv, s: (0, 0))
    return pl.pallas_call(
        functools.partial(_band_kernel, n_blocks=n_blocks, blocks_per_step=tq // BAND),
        grid=(nb, KV_B, nq),
        in_specs=[pl.BlockSpec(memory_space=pltpu.SMEM), heads, keys, vals, gates, ctx, ctx, tab, tab, tab,
                  pl.BlockSpec(mask.shape, lambda b, kv, s: (0, 0, 0))],
        out_specs=pl.BlockSpec((tq, G_B * HEAD_DIM), lambda b, kv, s: (b * nq + s, kv)),
        out_shape=jax.ShapeDtypeStruct((nb * t, W_B), BF16),
        scratch_shapes=[pltpu.VMEM((t, HEAD_DIM), BF16), pltpu.VMEM((t, HEAD_DIM), BF16)],
        compiler_params=_params(("arbitrary", "arbitrary", "arbitrary")),
        name="band_attn",
    )(sink, proj, proj, proj, proj, cache_k, cache_v, cos, s_up, s_dn, mask)


def _cd_kernel(uc, ucp, ucn, zc, pa, pap, pan, pb, pbp, pbn, zd, wpool, pscale, cw, cb, lng, lnb, wpw, bpw,
               o_ref, ue, he, cv, hn, *, tt, t_seq):
    i = pl.program_id(1)
    has_prev = i > 0
    has_next = i < pl.num_programs(1) - 1
    ext = tt + 2 * HALO

    for c in range(CPG):
        ue[c, 0:HALO, :] = jnp.where(has_prev, ucp[c], 0.0)
        ue[c, HALO:HALO + tt, :] = uc[c]
        ue[c, HALO + tt:ext, :] = jnp.where(has_next, ucn[c], 0.0)
        he[c, 0:HALO, :] = jnp.where(has_prev, pap[c] * jax.nn.sigmoid(pbp[c]), 0.0)
        he[c, HALO:HALO + tt, :] = pa[c] * jax.nn.sigmoid(pb[c])
        he[c, HALO + tt:ext, :] = jnp.where(has_next, pan[c] * jax.nn.sigmoid(pbn[c]), 0.0)

    rt = 64
    for r0 in range(0, tt, rt):
        edge_tile = r0 == 0 or r0 + rt == tt
        tpos = i * tt + r0 + lax.broadcasted_iota(jnp.int32, (rt, LANE), 0)
        for g, w in enumerate(POOL_WINDOWS):
            half = w // 2
            if edge_tile:
                cnt = (jnp.minimum(tpos + half, t_seq) - jnp.maximum(tpos - half, 0)).astype(F32)
            d = []
            for c in (2 * g, 2 * g + 1):
                acc = ue[c, HALO + r0 - half:HALO + r0 - half + rt, :]
                for o in range(-half + 1, half):
                    acc = acc + ue[c, HALO + r0 + o:HALO + r0 + o + rt, :]
                mean = acc / cnt if edge_tile else acc * (1.0 / w)
                d.append(mean - uc[c, r0:r0 + rt, :])
            dg = jnp.concatenate(d, axis=1).astype(BF16)
            cols = slice(g * POOL_GW, (g + 1) * POOL_GW)
            y = _dot(dg, wpool[g]) * pscale[:, cols]
            z = jnp.concatenate([zc[2 * g, r0:r0 + rt, :], zc[2 * g + 1, r0:r0 + rt, :]], axis=1)
            o_ref[r0:r0 + rt, cols] = (y * _silu(z)).astype(BF16)

    def conv_chunk(c, carry):
        wts = cw[c]
        bias = cb[c]
        for r0 in range(0, tt, rt):
            acc = jnp.zeros((rt, LANE), F32) + bias
            for k in range(CONV_K):
                off = HALO - CONV_K // 2 + k + r0
                acc = acc + he[c, off:off + rt, :] * wts[k:k + 1, :]
            cv[c, r0:r0 + rt, :] = acc
        return carry

    lax.fori_loop(0, CPG, conv_chunk, 0)

    def ln_rows(r, carry):
        r0 = pl.multiple_of(r * rt, rt)
        rs = pl.ds(r0, rt)
        xs = [cv[c, rs, :] for c in range(CPG)]
        tot = xs[0]
        for x in xs[1:]:
            tot = tot + x
        mu = jnp.sum(tot, axis=-1, keepdims=True) * (1.0 / W_D)
        sq = jnp.square(xs[0] - mu)
        for x in xs[1:]:
            sq = sq + jnp.square(x - mu)
        rstd = lax.rsqrt(jnp.sum(sq, axis=-1, keepdims=True) * (1.0 / W_D) + EPS)
        for c in range(CPG):
            cols = slice(c * LANE, (c + 1) * LANE)
            y = (xs[c] - mu) * rstd * lng[:, cols] + lnb[:, cols]
            hn[rs, cols] = _silu(y).astype(BF16)
        return carry

    lax.fori_loop(0, tt // rt, ln_rows, 0, unroll=2)

    for r0 in range(0, tt, 256):
        rs = slice(r0, r0 + 256)
        y = _dot(hn[rs, :], wpw[...]) + bpw[...]
        for c in range(CPG):
            cols = slice(c * LANE, (c + 1) * LANE)
            o_ref[rs, W_C + c * LANE:W_C + (c + 1) * LANE] = (y[:, cols] * _silu(zd[c, rs, :])).astype(BF16)


def _cd_mixers(proj, w_pool_bf, pool_scale, conv_w, conv_b, ln_g, ln_b, w_pw2_bf, b_pw2, nb, t, tt):
    m = nb * t
    nt = t // tt
    hb = tt // HALO
    last = m // HALO - 1

    def cur(g):
        return pl.BlockSpec((CPG, tt, LANE), lambda b, i: (g, b * nt + i, 0))

    def prev(g):
        return pl.BlockSpec((CPG, HALO, LANE), lambda b, i: (g, jnp.maximum((b * nt + i) * hb - 1, 0), 0))

    def nxt(g):
        return pl.BlockSpec((CPG, HALO, LANE), lambda b, i: (g, jnp.minimum((b * nt + i + 1) * hb, last), 0))

    def full(shape):
        return pl.BlockSpec(shape, lambda b, i: (0,) * len(shape))

    ext = tt + 2 * HALO
    cw = conv_w.reshape(CONV_K, CPG, LANE).transpose(1, 0, 2)
    cb = conv_b.reshape(CPG, 1, LANE)
    return pl.pallas_call(
        functools.partial(_cd_kernel, tt=tt, t_seq=t),
        grid=(nb, nt),
        in_specs=[cur(G_UC), prev(G_UC), nxt(G_UC), cur(G_ZC),
                  cur(G_PA), prev(G_PA), nxt(G_PA), cur(G_PB), prev(G_PB), nxt(G_PB), cur(G_ZD),
                  full((N_POOL, POOL_GW, POOL_GW)), full((1, W_C)), full((CPG, CONV_K, LANE)),
                  full((CPG, 1, LANE)), full((1, W_D)), full((1, W_D)), full((W_D, W_D)), full((1, W_D))],
        out_specs=pl.BlockSpec((tt, W_C + W_D), lambda b, i: (b * nt + i, 0)),
        out_shape=jax.ShapeDtypeStruct((m, W_C + W_D), BF16),
        scratch_shapes=[pltpu.VMEM((CPG, ext, LANE), F32), pltpu.VMEM((CPG, ext, LANE), F32),
                        pltpu.VMEM((CPG, tt, LANE), F32), pltpu.VMEM((tt, W_D), BF16)],
        compiler_params=_params(("arbitrary", "arbitrary")),
        name="cd_mixers",
    )(proj, proj, proj, proj, proj, proj, proj, proj, proj, proj, proj,
      w_pool_bf, pool_scale.reshape(1, W_C), cw, cb, ln_g.reshape(1, W_D), ln_b.reshape(1, W_D),
      w_pw2_bf, b_pw2.reshape(1, W_D))


def _outproj_kernel(*refs, tn, tm, last):
    if last:
        ya, yb, ycd, w_ref, x_ref, gate_ref, g_ref, y_ref, stage, ssq_ref = refs
    else:
        ya, yb, ycd, w_ref, x_ref, gate_ref, g_ref, sh_ref, sc_ref, xo_ref, h_ref, stage, ssq_ref = refs
    j = pl.program_id(1)

    @pl.when(j == 0)
    def _():
        ssq_ref[...] = jnp.zeros_like(ssq_ref)

    acc = _dot(ya[...], w_ref[0:W_A, :])
    acc = acc + _dot(yb[...], w_ref[W_A:W_A + W_B, :])
    acc = acc + _dot(ycd[...], w_ref[W_A + W_B:, :])
    x_new = x_ref[...] + gate_ref[...] * acc
    stage[:, pl.ds(pl.multiple_of(j * tn, tn), tn)] = x_new
    if not last:
        xo_ref[...] = x_new
    ssq_ref[...] += jnp.sum(x_new * x_new, axis=-1, keepdims=True)

    @pl.when(j == pl.num_programs(1) - 1)
    def _():
        if last:
            g = jnp.broadcast_to(g_ref[...], (NORM_ROWS, D_MODEL))

            def emit(rs, y):
                y_ref[rs, :] = y * g
        else:
            gs = jnp.broadcast_to(g_ref[...] * (1.0 + sc_ref[...]), (NORM_ROWS, D_MODEL))
            sh = jnp.broadcast_to(sh_ref[...], (NORM_ROWS, D_MODEL))

            def emit(rs, y):
                h_ref[rs, :] = (y * gs + sh).astype(BF16)

        _normalise_sweep(stage, ssq_ref, tm, emit)


def _outproj(ya, yb, ycd, w_bf, x, gate, norm, layer, tokens_per_cond, tm=OUTPROJ_TM, tn=OUTPROJ_TN):
    m = x.shape[0]
    tpc = tokens_per_cond // tm
    last = len(norm) == 1
    rows = pl.BlockSpec((tm, D_MODEL), lambda i, j: (i, 0))
    cond = pl.BlockSpec((None, 1, D_MODEL), lambda i, j: (i // tpc, 0, 0))
    in_specs = [
        pl.BlockSpec((tm, W_A), lambda i, j: (i, 0)),
        pl.BlockSpec((tm, W_B), lambda i, j: (i, 0)),
        pl.BlockSpec((tm, W_C + W_D), lambda i, j: (i, 0)),
        pl.BlockSpec((None, D_MODEL, tn), lambda i, j: (layer, 0, j)),
        pl.BlockSpec((tm, tn), lambda i, j: (i, j)),
        pl.BlockSpec((None, 1, tn), lambda i, j: (i // tpc, 0, j)),
        pl.BlockSpec((1, D_MODEL), lambda i, j: (0, 0)),
    ]
    args = [ya, yb, ycd, w_bf, x, gate, norm[0].reshape(1, D_MODEL)]
    scratch = [pltpu.VMEM((tm, D_MODEL), F32), pltpu.VMEM((tm, 1), F32)]
    if last:
        out_specs = rows
        out_shape = jax.ShapeDtypeStruct((m, D_MODEL), F32)
    else:
        in_specs += [cond, cond]
        args += [norm[1], norm[2]]
        out_specs = [pl.BlockSpec((tm, tn), lambda i, j: (i, j)), rows]
        out_shape = [jax.ShapeDtypeStruct((m, D_MODEL), F32), jax.ShapeDtypeStruct((m, D_MODEL), BF16)]
    return pl.pallas_call(
        functools.partial(_outproj_kernel, tn=tn, tm=tm, last=last),
        grid=(m // tm, D_MODEL // tn),
        in_specs=in_specs,
        out_specs=out_specs,
        out_shape=out_shape,
        scratch_shapes=scratch,
        compiler_params=_params(("arbitrary", "arbitrary")),
        name="outproj",
    )(*args)


def kernel(x_prompt, x_sample, cache_a_k, cache_a_v, cache_b_k, cache_b_v, c, c_ctx, norm_g, w_ada, b_ada, w_in,
           rpb_a, sink_b, w_pool, pool_scale, conv_w, conv_b, ln_g, ln_b, w_pw2, b_pw2, w_out, final_g):
    nbp, tp, _ = x_prompt.shape
    nbs, ts, _ = x_sample.shape
    p_len = cache_a_k.shape[2]
    assert ts == GRID_W * GRID_W and nbs + 1 <= 8

    w_in_bf = w_in.astype(BF16)
    w_out_bf = w_out.astype(BF16)
    w_pool_bf = w_pool.astype(BF16)
    w_pw2_bf = w_pw2.astype(BF16)
    cak = cache_a_k.reshape(nbs, DEPTH, p_len, W_A)
    cav = cache_a_v.reshape(nbs, DEPTH, p_len, W_A)
    cbk = cache_b_k.reshape(nbs, DEPTH, p_len, KV_B * HEAD_DIM)
    cbv = cache_b_v.reshape(nbs, DEPTH, p_len, KV_B * HEAD_DIM)

    cond8 = jnp.concatenate([c_ctx[None, :], c, jnp.zeros((8 - 1 - nbs, D_MODEL), F32)], axis=0)
    ada = _ada(cond8, w_ada, b_ada)
    rope = _rope_tables(ts)
    band_mask = _band_mask(ts)

    def mod(l, k, lo, hi):
        return ada[l, lo:hi, k * D_MODEL:(k + 1) * D_MODEL].reshape(hi - lo, 1, D_MODEL)

    def next_norm(l, lo, hi):
        if l == DEPTH - 1:
            return (final_g,)
        return (norm_g[l + 1], mod(l + 1, 0, lo, hi), mod(l + 1, 1, lo, hi))

    xp = x_prompt.reshape(nbp * tp, D_MODEL)
    xs = x_sample.reshape(nbs * ts, D_MODEL)
    hp = _prenorm(xp, norm_g[0], mod(0, 0, 0, 1), mod(0, 1, 0, 1), nbp * tp)
    hs = _prenorm(xs, norm_g[0], mod(0, 0, 1, 1 + nbs), mod(0, 1, 1, 1 + nbs), ts)
    kv_out = [[], [], [], []]
    for l in range(DEPTH):
        na_bias, na_mask = _na_tables(rpb_a[l])

        proj = _inproj(hp, w_in_bf, l)
        ya, yb, nak, nav, nbk, nbv = _ctx_attn(proj, sink_b[l], nbp, tp)
        ycd = _cd_mixers(proj, w_pool_bf[l], pool_scale[l], conv_w[l], conv_b[l], ln_g[l], ln_b[l],
                         w_pw2_bf[l], b_pw2[l], nbp, tp, tt=min(tp, CD_TT))
        res = _outproj(ya, yb, ycd, w_out_bf, xp, mod(l, 2, 0, 1), next_norm(l, 0, 1), l, nbp * tp)
        if l == DEPTH - 1:
            y_prompt = res.reshape(nbp, tp, D_MODEL)
        else:
            xp, hp = res
        for lst, a in zip(kv_out, (nak, nav, nbk, nbv)):
            lst.append(a)

        proj = _inproj(hs, w_in_bf, l)
        ya = _na_attn(proj, cak, cav, na_bias, na_mask, l, nbs, ts)
        yb = _band_attn(proj, sink_b[l], cbk, cbv, rope, band_mask, l, nbs, ts)
        ycd = _cd_mixers(proj, w_pool_bf[l], pool_scale[l], conv_w[l], conv_b[l], ln_g[l], ln_b[l],
                         w_pw2_bf[l], b_pw2[l], nbs, ts, tt=min(ts, CD_TT))
        res = _outproj(ya, yb, ycd, w_out_bf, xs, mod(l, 2, 1, 1 + nbs), next_norm(l, 1, 1 + nbs), l, ts)
        if l == DEPTH - 1:
            y_sample = res.reshape(nbs, ts, D_MODEL)
        else:
            xs, hs = res

    new_a_k = jnp.stack(kv_out[0], axis=1).reshape(nbp, DEPTH, tp, H_A, HEAD_DIM)
    new_a_v = jnp.stack(kv_out[1], axis=1).reshape(nbp, DEPTH, tp, H_A, HEAD_DIM)
    new_b_k = jnp.stack(kv_out[2], axis=1).reshape(nbp, DEPTH, tp, KV_B, HEAD_DIM)
    new_b_v = jnp.stack(kv_out[3], axis=1).reshape(nbp, DEPTH, tp, KV_B, HEAD_DIM)
    return (y_prompt, y_sample, new_a_k, new_a_v, new_b_k, new_b_v)
```

```python
import functools

import numpy as np
import jax
import jax.numpy as jnp
from jax import lax
from jax.experimental import pallas as pl
from jax.experimental.pallas import tpu as pltpu

D_MODEL = 4096
DEPTH = 2
GRID_W = 64
HEAD_DIM = 128
W_A = D_MODEL // 4
W_B = D_MODEL // 4
W_C = D_MODEL // 4
W_D = D_MODEL - W_A - W_B - W_C
H_A = W_A // HEAD_DIM
H_B = W_B // HEAD_DIM
KV_B = max(1, H_B // 4)
G_B = H_B // KV_B
NA_ROWS = 8
NA_COLS = 16
BAND = 128
N_POOL = 4
POOL_WINDOWS = (2, 4, 8, 16)
POOL_GW = W_C // N_POOL
CONV_K = 31
ROPE_BASE = 10000.0
EPS = 1e-6
NEG = -1e30
LOG2E = float(np.log2(np.e))
QK_SCALE_LOG2 = HEAD_DIM ** -0.5 * LOG2E
IN_W = 4 * W_A + 2 * W_B + 2 * KV_B * HEAD_DIM + 2 * W_C + 3 * W_D

LANE = 128
N_CHUNK = IN_W // LANE
CPG = W_A // LANE
G_QA, G_KA, G_VA, G_ZA, G_QB, G_ZB, G_UC, G_ZC, G_PA, G_PB, G_ZD = range(11)
C_KB = 11 * CPG
C_VB = C_KB + KV_B
HALO = 16
NA_QR = 4
NA_KR = 12
VMEM_LIMIT = 56 * 1024 * 1024
ADA_TN = 512
PRENORM_TM = 256
INPROJ_TM, INPROJ_TN = 2048, 512
OUTPROJ_TM, OUTPROJ_TN = 512, 1024
CD_TT = 512
BAND_TQ = 1024

F32 = jnp.float32
BF16 = jnp.bfloat16


def _silu(x):
    return x * jax.nn.sigmoid(x)


def _dot(a, b):
    return jnp.dot(a, b, preferred_element_type=F32)


def _dot_nt(a, b):
    return lax.dot_general(a, b, (((1,), (1,)), ((), ())), preferred_element_type=F32)


def _params(sem):
    return pltpu.CompilerParams(dimension_semantics=sem, vmem_limit_bytes=VMEM_LIMIT)


def _ada_kernel(c_ref, w_ref, b_ref, o_ref):
    s = _silu(c_ref[...])
    o_ref[...] = _dot(s.astype(BF16), w_ref[...].astype(BF16)) + b_ref[...]


def _ada(cond8, w_ada, b_ada):
    tn = ADA_TN
    n = 3 * D_MODEL
    return pl.pallas_call(
        _ada_kernel,
        grid=(DEPTH, n // tn),
        in_specs=[
            pl.BlockSpec((8, D_MODEL), lambda l, j: (0, 0)),
            pl.BlockSpec((None, D_MODEL, tn), lambda l, j: (l, 0, j)),
            pl.BlockSpec((None, 1, tn), lambda l, j: (l, 0, j)),
        ],
        out_specs=pl.BlockSpec((None, 8, tn), lambda l, j: (l, 0, j)),
        out_shape=jax.ShapeDtypeStruct((DEPTH, 8, n), F32),
        compiler_params=_params(("arbitrary", "arbitrary")),
        name="ada",
    )(cond8, w_ada, b_ada.reshape(DEPTH, 1, n))


NORM_ROWS = 8


def _norm_tile(r):
    return pl.ds(pl.multiple_of(r * NORM_ROWS, NORM_ROWS), NORM_ROWS)


def _sumsq_sweep(src_ref, ssq_ref, n_rows):
    def stats(r, carry):
        x = src_ref[_norm_tile(r), :]
        ssq_ref[_norm_tile(r), :] = jnp.sum(x * x, axis=-1, keepdims=True)
        return carry

    lax.fori_loop(0, n_rows // NORM_ROWS, stats, 0, unroll=8)


def _normalise_sweep(src_ref, ssq_ref, n_rows, emit):
    def apply(r, carry):
        rstd = lax.rsqrt(ssq_ref[_norm_tile(r), :] * (1.0 / D_MODEL) + EPS)
        emit(_norm_tile(r), src_ref[_norm_tile(r), :] * rstd)
        return carry

    lax.fori_loop(0, n_rows // NORM_ROWS, apply, 0, unroll=8)


def _prenorm_kernel(x_ref, g_ref, sh_ref, sc_ref, h_ref, ssq_ref, *, tm):
    gs = jnp.broadcast_to(g_ref[...] * (1.0 + sc_ref[...]), (NORM_ROWS, D_MODEL))
    sh = jnp.broadcast_to(sh_ref[...], (NORM_ROWS, D_MODEL))

    def emit(rs, y):
        h_ref[rs, :] = (y * gs + sh).astype(BF16)

    _sumsq_sweep(x_ref, ssq_ref, tm)
    _normalise_sweep(x_ref, ssq_ref, tm, emit)


def _prenorm(x, norm_g, shift, scale, tokens_per_cond, tm=PRENORM_TM):
    m = x.shape[0]
    tpc = tokens_per_cond // tm
    cond = pl.BlockSpec((None, 1, D_MODEL), lambda i: (i // tpc, 0, 0))
    return pl.pallas_call(
        functools.partial(_prenorm_kernel, tm=tm),
        grid=(m // tm,),
        in_specs=[pl.BlockSpec((tm, D_MODEL), lambda i: (i, 0)), pl.BlockSpec((1, D_MODEL), lambda i: (0, 0)),
                  cond, cond],
        out_specs=pl.BlockSpec((tm, D_MODEL), lambda i: (i, 0)),
        out_shape=jax.ShapeDtypeStruct((m, D_MODEL), BF16),
        scratch_shapes=[pltpu.VMEM((tm, 1), F32)],
        compiler_params=_params(("arbitrary",)),
        name="prenorm",
    )(x, norm_g.reshape(1, D_MODEL), shift, scale)


def _inproj_kernel(h_ref, w_ref, o_ref, *, tn):
    acc = _dot(h_ref[...], w_ref[...])
    for s in range(tn // LANE):
        o_ref[s] = acc[:, s * LANE:(s + 1) * LANE]


def _inproj(h, w_bf, layer, tm=INPROJ_TM, tn=INPROJ_TN):
    m = h.shape[0]
    n_tiles = IN_W // tn
    kv0 = (4 * W_A + W_B) // tn
    n_kv = 2 * KV_B * HEAD_DIM // tn

    def src(j):
        return jnp.where(j < kv0, j, jnp.where(j < n_tiles - n_kv, j + n_kv, j - (n_tiles - n_kv) + kv0))

    return pl.pallas_call(
        functools.partial(_inproj_kernel, tn=tn),
        grid=(m // tm, n_tiles),
        in_specs=[
            pl.BlockSpec((tm, D_MODEL), lambda i, j: (i, 0)),
            pl.BlockSpec((None, D_MODEL, tn), lambda i, j: (layer, 0, src(j))),
        ],
        out_specs=pl.BlockSpec((tn // LANE, tm, LANE), lambda i, j: (j, i, 0)),
        out_shape=jax.ShapeDtypeStruct((N_CHUNK, m, LANE), F32),
        compiler_params=_params(("arbitrary", "arbitrary")),
        name="inproj",
    )(h, w_bf)


def _ctx_attn_kernel(sink_ref, qa, ka, va, za, qb, zb, kb, vb, ya, yb, nak, nav, nbk, nbv):
    for h in range(H_A):
        cols = slice(h * HEAD_DIM, (h + 1) * HEAD_DIM)
        k32 = ka[h]
        v32 = va[h]
        s = _dot_nt((qa[h] * QK_SCALE_LOG2).astype(BF16), k32.astype(BF16))
        p = jnp.exp2(s - jnp.max(s, axis=-1, keepdims=True))
        o = _dot(p.astype(BF16), v32.astype(BF16)) / jnp.sum(p, axis=-1, keepdims=True)
        ya[:, cols] = (o * _silu(za[h])).astype(BF16)
        nak[:, cols] = k32
        nav[:, cols] = v32
    for kv in range(KV_B):
        cols = slice(kv * HEAD_DIM, (kv + 1) * HEAD_DIM)
        nbk[:, cols] = kb[kv]
        nbv[:, cols] = vb[kv]
    for h in range(H_B):
        cols = slice(h * HEAD_DIM, (h + 1) * HEAD_DIM)
        kv = h // G_B
        sink = sink_ref[h] * LOG2E
        s = _dot_nt((qb[h] * QK_SCALE_LOG2).astype(BF16), kb[kv].astype(BF16))
        m = jnp.maximum(jnp.max(s, axis=-1, keepdims=True), sink)
        p = jnp.exp2(s - m)
        l = jnp.sum(p, axis=-1, keepdims=True) + jnp.exp2(sink - m)
        o = _dot(p.astype(BF16), vb[kv].astype(BF16)) / l
        yb[:, cols] = (o * _silu(zb[h])).astype(BF16)


def _ctx_attn(proj, sink, nb, t):
    m = nb * t

    def grp(g):
        return pl.BlockSpec((CPG, t, LANE), lambda b: (g, b, 0))

    def kvs(c):
        return pl.BlockSpec((KV_B, t, LANE), lambda b: (c // KV_B, b, 0))

    wide = pl.BlockSpec((t, W_A), lambda b: (b, 0))
    wide3 = pl.BlockSpec((None, t, W_A), lambda b: (b, 0, 0))
    narrow3 = pl.BlockSpec((None, t, KV_B * HEAD_DIM), lambda b: (b, 0, 0))
    return pl.pallas_call(
        _ctx_attn_kernel,
        grid=(nb,),
        in_specs=[pl.BlockSpec(memory_space=pltpu.SMEM),
                  grp(G_QA), grp(G_KA), grp(G_VA), grp(G_ZA), grp(G_QB), grp(G_ZB), kvs(C_KB), kvs(C_VB)],
        out_specs=[wide, wide, wide3, wide3, narrow3, narrow3],
        out_shape=[jax.ShapeDtypeStruct((m, W_A), BF16), jax.ShapeDtypeStruct((m, W_B), BF16),
                   jax.ShapeDtypeStruct((nb, t, W_A), F32), jax.ShapeDtypeStruct((nb, t, W_A), F32),
                   jax.ShapeDtypeStruct((nb, t, KV_B * HEAD_DIM), F32),
                   jax.ShapeDtypeStruct((nb, t, KV_B * HEAD_DIM), F32)],
        compiler_params=_params(("arbitrary",)),
        name="ctx_attn",
    )(sink, proj, proj, proj, proj, proj, proj, proj, proj)


def _na_bias_kernel(t_ref, o_ref, *, dr):
    for ty in range(dr.shape[0]):
        for qi in range(NA_QR):
            rows = slice(qi * GRID_W, (qi + 1) * GRID_W)
            for jp in range(NA_KR // 2):
                pair = jnp.concatenate([t_ref[int(dr[ty, qi, 2 * jp])], t_ref[int(dr[ty, qi, 2 * jp + 1])]], axis=1)
                o_ref[ty, rows, 2 * jp * GRID_W:(2 * jp + 2) * GRID_W] = pair


def _na_tables(rpb):
    rows = GRID_W
    n_dr = 2 * NA_ROWS - 1
    n_dc = 2 * NA_COLS - 1
    c = np.arange(GRID_W)[:, None]
    kc = np.arange(GRID_W)[None, :]
    start = np.clip(c - NA_COLS // 2, 0, GRID_W - NA_COLS)
    col_ok = (kc >= start) & (kc < start + NA_COLS)
    dc = np.clip(kc - c + NA_COLS - 1, 0, n_dc - 1)
    onehot = (dc[None] == np.arange(n_dc)[:, None, None]).astype(np.float32)
    toep = jnp.einsum("hrd,dck->hrck", rpb * LOG2E, jnp.asarray(onehot), precision=lax.Precision.HIGHEST)
    i = np.arange(NA_QR)[:, None]
    j = np.arange(NA_KR)[None, :]
    masks, drs = [], []
    for r0, ks in ((0, 0), (NA_QR, 0), (rows - NA_QR, rows - NA_KR)):
        r = r0 + i
        rs = np.clip(r - NA_ROWS // 2, 0, rows - NA_ROWS)
        kr = ks + j
        row_ok = (kr >= rs) & (kr < rs + NA_ROWS)
        masks.append(row_ok[:, None, :, None] & col_ok[None, :, None, :])
        drs.append(np.clip(kr - r + NA_ROWS - 1, 0, n_dr - 1))
    flat = (NA_QR * GRID_W, NA_KR * GRID_W)
    mask = np.stack([mk.reshape(flat) for mk in masks]).astype(np.float32)
    bias = pl.pallas_call(
        functools.partial(_na_bias_kernel, dr=np.stack(drs)),
        grid=(H_A,),
        in_specs=[pl.BlockSpec((None, n_dr, GRID_W, GRID_W), lambda h: (h, 0, 0, 0))],
        out_specs=pl.BlockSpec((None, len(drs)) + flat, lambda h: (h, 0, 0, 0)),
        out_shape=jax.ShapeDtypeStruct((H_A, len(drs)) + flat, F32),
        compiler_params=_params(("arbitrary",)),
        name="na_bias",
    )(toep)
    return bias, jnp.asarray(mask)


def _na_kernel(q_ref, k_ref, v_ref, z_ref, ck_ref, cv_ref, bias_ref, mask_ref, o_ref, *, n_blocks):
    qn = NA_QR * GRID_W
    kn = NA_KR * GRID_W
    ck = ck_ref[...].astype(BF16)
    cv = cv_ref[...].astype(BF16)

    def body(rb, carry):
        q0 = pl.multiple_of(rb * qn, qn)
        ks = jnp.clip(rb * NA_QR - NA_ROWS // 2, 0, n_blocks * NA_QR - NA_KR)
        k0 = pl.multiple_of(ks * GRID_W, GRID_W)
        ty = jnp.where(rb == 0, 0, jnp.where(rb == n_blocks - 1, 2, 1))
        q = (q_ref[pl.ds(q0, qn), :] * QK_SCALE_LOG2).astype(BF16)
        k = k_ref[pl.ds(k0, kn), :].astype(BF16)
        v = v_ref[pl.ds(k0, kn), :].astype(BF16)
        s_loc = _dot_nt(q, k) + bias_ref[ty]
        s_loc = jnp.where(mask_ref[ty] > 0.5, s_loc, NEG)
        s_ctx = _dot_nt(q, ck)
        m = jnp.maximum(jnp.max(s_loc, axis=-1, keepdims=True), jnp.max(s_ctx, axis=-1, keepdims=True))
        p_loc = jnp.exp2(s_loc - m)
        p_ctx = jnp.exp2(s_ctx - m)
        l = jnp.sum(p_loc, axis=-1, keepdims=True) + jnp.sum(p_ctx, axis=-1, keepdims=True)
        o = (_dot(p_loc.astype(BF16), v) + _dot(p_ctx.astype(BF16), cv)) / l
        o_ref[pl.ds(q0, qn), :] = (o * _silu(z_ref[pl.ds(q0, qn), :])).astype(BF16)
        return carry

    lax.fori_loop(0, n_blocks, body, 0, unroll=4)


def _na_attn(proj, cache_k, cache_v, bias, mask, layer, nb, t):
    p_len = cache_k.shape[2]
    n_blocks = t // (NA_QR * GRID_W)

    def head(g):
        return pl.BlockSpec((None, t, LANE), lambda b, h: (g * CPG + h, b, 0))

    ctx = pl.BlockSpec((None, None, p_len, HEAD_DIM), lambda b, h: (b, layer, 0, h))
    return pl.pallas_call(
        functools.partial(_na_kernel, n_blocks=n_blocks),
        grid=(nb, H_A),
        in_specs=[head(G_QA), head(G_KA), head(G_VA), head(G_ZA), ctx, ctx,
                  pl.BlockSpec((None,) + bias.shape[1:], lambda b, h: (h, 0, 0, 0)),
                  pl.BlockSpec(mask.shape, lambda b, h: (0, 0, 0))],
        out_specs=pl.BlockSpec((t, HEAD_DIM), lambda b, h: (b, h)),
        out_shape=jax.ShapeDtypeStruct((nb * t, W_A), BF16),
        compiler_params=_params(("arbitrary", "arbitrary")),
        name="na_attn",
    )(proj, proj, proj, proj, cache_k, cache_v, bias, mask)


def _rope_tables(t):
    pos = np.arange(t)
    half = HEAD_DIM // 2
    inv = 1.0 / (ROPE_BASE ** (np.arange(0, half, 2, dtype=np.float64) / half))
    ang_r = (pos // GRID_W)[:, None] * inv[None, :]
    ang_c = (pos % GRID_W)[:, None] * inv[None, :]
    zero = np.zeros_like(ang_r)
    cos = np.concatenate([np.cos(ang_r), np.cos(ang_r), np.cos(ang_c), np.cos(ang_c)], axis=1)
    s_up = np.concatenate([-np.sin(ang_r), zero, -np.sin(ang_c), zero], axis=1)
    s_dn = np.concatenate([zero, np.sin(ang_r), zero, np.sin(ang_c)], axis=1)
    return tuple(jnp.asarray(a.astype(np.float32)) for a in (cos, s_up, s_dn))


def _band_mask(t):
    row = np.tile(np.arange(BAND), G_B)[:, None]
    col = np.arange(3 * BAND)[None, :]
    out = []
    for off in (0, -BAND, -2 * BAND):
        out.append(np.abs(off + col - row) <= BAND)
    return jnp.asarray(np.stack(out).astype(np.float32))


def _rope(x, cos, s_up, s_dn):
    quarter = HEAD_DIM // 4
    return x * cos + pltpu.roll(x, HEAD_DIM - quarter, 1) * s_up + pltpu.roll(x, quarter, 1) * s_dn


def _band_kernel(sink_ref, q_ref, k_ref, v_ref, z_ref, ck_ref, cv_ref, cos_ref, up_ref, dn_ref, mask_ref,
                 o_ref, kr_ref, vr_ref, *, n_blocks, blocks_per_step):
    kv = pl.program_id(1)
    step = pl.program_id(2)
    t = n_blocks * BAND
    chunk = 512

    @pl.when(step == 0)
    def _():
        def prep(i, carry):
            r0 = pl.multiple_of(i * chunk, chunk)
            sl = pl.ds(r0, chunk)
            kr_ref[sl, :] = _rope(k_ref[sl, :], cos_ref[sl, :], up_ref[sl, :], dn_ref[sl, :]).astype(BF16)
            vr_ref[sl, :] = v_ref[sl, :].astype(BF16)
            return carry

        lax.fori_loop(0, t // chunk, prep, 0)

    ck = ck_ref[...].astype(BF16)
    cv = cv_ref[...].astype(BF16)
    sink = jnp.concatenate([jnp.full((BAND, 1), sink_ref[kv * G_B + g] * LOG2E, F32) for g in range(G_B)], axis=0)

    def body(j, carry):
        i = step * blocks_per_step + j
        ql = pl.ds(pl.multiple_of(j * BAND, BAND), BAND)
        qs = pl.ds(pl.multiple_of(i * BAND, BAND), BAND)
        kb = jnp.clip(i - 1, 0, n_blocks - 3)
        k0 = pl.multiple_of(kb * BAND, BAND)
        ty = jnp.where(i == 0, 0, jnp.where(i == n_blocks - 1, 2, 1))
        cos = cos_ref[qs, :]
        up = up_ref[qs, :]
        dn = dn_ref[qs, :]
        q = jnp.concatenate([(_rope(q_ref[g, ql, :], cos, up, dn) * QK_SCALE_LOG2).astype(BF16)
                             for g in range(G_B)], axis=0)
        k = kr_ref[pl.ds(k0, 3 * BAND), :]
        v = vr_ref[pl.ds(k0, 3 * BAND), :]
        s_loc = jnp.where(mask_ref[ty] > 0.5, _dot_nt(q, k), NEG)
        s_ctx = _dot_nt(q, ck)
        m = jnp.maximum(jnp.maximum(jnp.max(s_loc, axis=-1, keepdims=True),
                                    jnp.max(s_ctx, axis=-1, keepdims=True)), sink)
        p_loc = jnp.exp2(s_loc - m)
        p_ctx = jnp.exp2(s_ctx - m)
        l = jnp.sum(p_loc, axis=-1, keepdims=True) + jnp.sum(p_ctx, axis=-1, keepdims=True) + jnp.exp2(sink - m)
        o = (_dot(p_loc.astype(BF16), v) + _dot(p_ctx.astype(BF16), cv)) / l
        for g in range(G_B):
            og = o[g * BAND:(g + 1) * BAND, :] * _silu(z_ref[g, ql, :])
            o_ref[ql, g * HEAD_DIM:(g + 1) * HEAD_DIM] = og.astype(BF16)
        return carry

    lax.fori_loop(0, blocks_per_step, body, 0, unroll=2)


def _band_attn(proj, sink, cache_k, cache_v, rope, mask, layer, nb, t, tq=BAND_TQ):
    p_len = cache_k.shape[2]
    n_blocks = t // BAND
    nq = t // tq
    cos, s_up, s_dn = rope
    heads = pl.BlockSpec((G_B, tq, LANE), lambda b, kv, s: (G_QB * CPG // G_B + kv, b * nq + s, 0))
    gates = pl.BlockSpec((G_B, tq, LANE), lambda b, kv, s: (G_ZB * CPG // G_B + kv, b * nq + s, 0))
    keys = pl.BlockSpec((None, t, LANE), lambda b, kv, s: (C_KB + kv, b, 0))
    vals = pl.BlockSpec((None, t, LANE), lambda b, kv, s: (C_VB + kv, b, 0))
    ctx = pl.BlockSpec((None, None, p_len, HEAD_DIM), lambda b, kv, s: (b, layer, 0, kv))
    tab = pl.BlockSpec((t, HEAD_DIM), lambda b, kv, s: (0, 0))
    return pl.pallas_call(
        functools.partial(_band_kernel, n_blocks=n_blocks, blocks_per_step=tq // BAND),
        grid=(nb, KV_B, nq),
        in_specs=[pl.BlockSpec(memory_space=pltpu.SMEM), heads, keys, vals, gates, ctx, ctx, tab, tab, tab,
                  pl.BlockSpec(mask.shape, lambda b, kv, s: (0, 0, 0))],
        out_specs=pl.BlockSpec((tq, G_B * HEAD_DIM), lambda b, kv, s: (b * nq + s, kv)),
        out_shape=jax.ShapeDtypeStruct((nb * t, W_B), BF16),
        scratch_shapes=[pltpu.VMEM((t, HEAD_DIM), BF16), pltpu.VMEM((t, HEAD_DIM), BF16)],
        compiler_params=_params(("arbitrary", "arbitrary", "arbitrary")),
        name="band_attn",
    )(sink, proj, proj, proj, proj, cache_k, cache_v, cos, s_up, s_dn, mask)


def _cd_kernel(uc, ucp, ucn, zc, pa, pap, pan, pb, pbp, pbn, zd, wpool, pscale, cw, cb, lng, lnb, wpw, bpw,
               o_ref, ue, he, cv, hn, *, tt, t_seq):
    i = pl.program_id(1)
    has_prev = i > 0
    has_next = i < pl.num_programs(1) - 1
    ext = tt + 2 * HALO

    for c in range(CPG):
        ue[c, 0:HALO, :] = jnp.where(has_prev, ucp[c], 0.0)
        ue[c, HALO:HALO + tt, :] = uc[c]
        ue[c, HALO + tt:ext, :] = jnp.where(has_next, ucn[c], 0.0)
        he[c, 0:HALO, :] = jnp.where(has_prev, pap[c] * jax.nn.sigmoid(pbp[c]), 0.0)
        he[c, HALO:HALO + tt, :] = pa[c] * jax.nn.sigmoid(pb[c])
        he[c, HALO + tt:ext, :] = jnp.where(has_next, pan[c] * jax.nn.sigmoid(pbn[c]), 0.0)

    rt = 64
    for r0 in range(0, tt, rt):
        edge_tile = r0 == 0 or r0 + rt == tt
        tpos = i * tt + r0 + lax.broadcasted_iota(jnp.int32, (rt, LANE), 0)
        for g, w in enumerate(POOL_WINDOWS):
            half = w // 2
            if edge_tile:
                cnt = (jnp.minimum(tpos + half, t_seq) - jnp.maximum(tpos - half, 0)).astype(F32)
            d = []
            for c in (2 * g, 2 * g + 1):
                acc = ue[c, HALO + r0 - half:HALO + r0 - half + rt, :]
                for o in range(-half + 1, half):
                    acc = acc + ue[c, HALO + r0 + o:HALO + r0 + o + rt, :]
                mean = acc / cnt if edge_tile else acc * (1.0 / w)
                d.append(mean - uc[c, r0:r0 + rt, :])
            dg = jnp.concatenate(d, axis=1).astype(BF16)
            cols = slice(g * POOL_GW, (g + 1) * POOL_GW)
            y = _dot(dg, wpool[g]) * pscale[:, cols]
            z = jnp.concatenate([zc[2 * g, r0:r0 + rt, :], zc[2 * g + 1, r0:r0 + rt, :]], axis=1)
            o_ref[r0:r0 + rt, cols] = (y * _silu(z)).astype(BF16)

    def conv_chunk(c, carry):
        wts = cw[c]
        bias = cb[c]
        for r0 in range(0, tt, rt):
            acc = jnp.zeros((rt, LANE), F32) + bias
            for k in range(CONV_K):
                off = HALO - CONV_K // 2 + k + r0
                acc = acc + he[c, off:off + rt, :] * wts[k:k + 1, :]
            cv[c, r0:r0 + rt, :] = acc
        return carry

    lax.fori_loop(0, CPG, conv_chunk, 0)

    def ln_rows(r, carry):
        r0 = pl.multiple_of(r * rt, rt)
        rs = pl.ds(r0, rt)
        xs = [cv[c, rs, :] for c in range(CPG)]
        tot = xs[0]
        for x in xs[1:]:
            tot = tot + x
        mu = jnp.sum(tot, axis=-1, keepdims=True) * (1.0 / W_D)
        sq = jnp.square(xs[0] - mu)
        for x in xs[1:]:
            sq = sq + jnp.square(x - mu)
        rstd = lax.rsqrt(jnp.sum(sq, axis=-1, keepdims=True) * (1.0 / W_D) + EPS)
        for c in range(CPG):
            cols = slice(c * LANE, (c + 1) * LANE)
            y = (xs[c] - mu) * rstd * lng[:, cols] + lnb[:, cols]
            hn[rs, cols] = _silu(y).astype(BF16)
        return carry

    lax.fori_loop(0, tt // rt, ln_rows, 0, unroll=2)

    for r0 in range(0, tt, 256):
        rs = slice(r0, r0 + 256)
        y = _dot(hn[rs, :], wpw[...]) + bpw[...]
        for c in range(CPG):
            cols = slice(c * LANE, (c + 1) * LANE)
            o_ref[rs, W_C + c * LANE:W_C + (c + 1) * LANE] = (y[:, cols] * _silu(zd[c, rs, :])).astype(BF16)


def _cd_mixers(proj, w_pool_bf, pool_scale, conv_w, conv_b, ln_g, ln_b, w_pw2_bf, b_pw2, nb, t, tt):
    m = nb * t
    nt = t // tt
    hb = tt // HALO
    last = m // HALO - 1

    def cur(g):
        return pl.BlockSpec((CPG, tt, LANE), lambda b, i: (g, b * nt + i, 0))

    def prev(g):
        return pl.BlockSpec((CPG, HALO, LANE), lambda b, i: (g, jnp.maximum((b * nt + i) * hb - 1, 0), 0))

    def nxt(g):
        return pl.BlockSpec((CPG, HALO, LANE), lambda b, i: (g, jnp.minimum((b * nt + i + 1) * hb, last), 0))

    def full(shape):
        return pl.BlockSpec(shape, lambda b, i: (0,) * len(shape))

    ext = tt + 2 * HALO
    cw = conv_w.reshape(CONV_K, CPG, LANE).transpose(1, 0, 2)
    cb = conv_b.reshape(CPG, 1, LANE)
    return pl.pallas_call(
        functools.partial(_cd_kernel, tt=tt, t_seq=t),
        grid=(nb, nt),
        in_specs=[cur(G_UC), prev(G_UC), nxt(G_UC), cur(G_ZC),
                  cur(G_PA), prev(G_PA), nxt(G_PA), cur(G_PB), prev(G_PB), nxt(G_PB), cur(G_ZD),
                  full((N_POOL, POOL_GW, POOL_GW)), full((1, W_C)), full((CPG, CONV_K, LANE)),
                  full((CPG, 1, LANE)), full((1, W_D)), full((1, W_D)), full((W_D, W_D)), full((1, W_D))],
        out_specs=pl.BlockSpec((tt, W_C + W_D), lambda b, i: (b * nt + i, 0)),
        out_shape=jax.ShapeDtypeStruct((m, W_C + W_D), BF16),
        scratch_shapes=[pltpu.VMEM((CPG, ext, LANE), F32), pltpu.VMEM((CPG, ext, LANE), F32),
                        pltpu.VMEM((CPG, tt, LANE), F32), pltpu.VMEM((tt, W_D), BF16)],
        compiler_params=_params(("arbitrary", "arbitrary")),
        name="cd_mixers",
    )(proj, proj, proj, proj, proj, proj, proj, proj, proj, proj, proj,
      w_pool_bf, pool_scale.reshape(1, W_C), cw, cb, ln_g.reshape(1, W_D), ln_b.reshape(1, W_D),
      w_pw2_bf, b_pw2.reshape(1, W_D))


def _outproj_kernel(*refs, tn, tm, last):
    if last:
        ya, yb, ycd, w_ref, x_ref, gate_ref, g_ref, y_ref, stage, ssq_ref = refs
    else:
        ya, yb, ycd, w_ref, x_ref, gate_ref, g_ref, sh_ref, sc_ref, xo_ref, h_ref, stage, ssq_ref = refs
    j = pl.program_id(1)

    @pl.when(j == 0)
    def _():
        ssq_ref[...] = jnp.zeros_like(ssq_ref)

    acc = _dot(ya[...], w_ref[0:W_A, :])
    acc = acc + _dot(yb[...], w_ref[W_A:W_A + W_B, :])
    acc = acc + _dot(ycd[...], w_ref[W_A + W_B:, :])
    x_new = x_ref[...] + gate_ref[...] * acc
    stage[:, pl.ds(pl.multiple_of(j * tn, tn), tn)] = x_new
    if not last:
        xo_ref[...] = x_new
    ssq_ref[...] += jnp.sum(x_new * x_new, axis=-1, keepdims=True)

    @pl.when(j == pl.num_programs(1) - 1)
    def _():
        if last:
            g = jnp.broadcast_to(g_ref[...], (NORM_ROWS, D_MODEL))

            def emit(rs, y):
                y_ref[rs, :] = y * g
        else:
            gs = jnp.broadcast_to(g_ref[...] * (1.0 + sc_ref[...]), (NORM_ROWS, D_MODEL))
            sh = jnp.broadcast_to(sh_ref[...], (NORM_ROWS, D_MODEL))

            def emit(rs, y):
                h_ref[rs, :] = (y * gs + sh).astype(BF16)

        _normalise_sweep(stage, ssq_ref, tm, emit)


def _outproj(ya, yb, ycd, w_bf, x, gate, norm, layer, tokens_per_cond, tm=OUTPROJ_TM, tn=OUTPROJ_TN):
    m = x.shape[0]
    tpc = tokens_per_cond // tm
    last = len(norm) == 1
    rows = pl.BlockSpec((tm, D_MODEL), lambda i, j: (i, 0))
    cond = pl.BlockSpec((None, 1, D_MODEL), lambda i, j: (i // tpc, 0, 0))
    in_specs = [
        pl.BlockSpec((tm, W_A), lambda i, j: (i, 0)),
        pl.BlockSpec((tm, W_B), lambda i, j: (i, 0)),
        pl.BlockSpec((tm, W_C + W_D), lambda i, j: (i, 0)),
        pl.BlockSpec((None, D_MODEL, tn), lambda i, j: (layer, 0, j)),
        pl.BlockSpec((tm, tn), lambda i, j: (i, j)),
        pl.BlockSpec((None, 1, tn), lambda i, j: (i // tpc, 0, j)),
        pl.BlockSpec((1, D_MODEL), lambda i, j: (0, 0)),
    ]
    args = [ya, yb, ycd, w_bf, x, gate, norm[0].reshape(1, D_MODEL)]
    scratch = [pltpu.VMEM((tm, D_MODEL), F32), pltpu.VMEM((tm, 1), F32)]
    if last:
        out_specs = rows
        out_shape = jax.ShapeDtypeStruct((m, D_MODEL), F32)
    else:
        in_specs += [cond, cond]
        args += [norm[1], norm[2]]
        out_specs = [pl.BlockSpec((tm, tn), lambda i, j: (i, j)), rows]
        out_shape = [jax.ShapeDtypeStruct((m, D_MODEL), F32), jax.ShapeDtypeStruct((m, D_MODEL), BF16)]
    return pl.pallas_call(
        functools.partial(_outproj_kernel, tn=tn, tm=tm, last=last),
        grid=(m // tm, D_MODEL // tn),
        in_specs=in_specs,
        out_specs=out_specs,
        out_shape=out_shape,
        scratch_shapes=scratch,
        compiler_params=_params(("arbitrary", "arbitrary")),
        name="outproj",
    )(*args)


def kernel(x_prompt, x_sample, cache_a_k, cache_a_v, cache_b_k, cache_b_v, c, c_ctx, norm_g, w_ada, b_ada, w_in,
           rpb_a, sink_b, w_pool, pool_scale, conv_w, conv_b, ln_g, ln_b, w_pw2, b_pw2, w_out, final_g):
    nbp, tp, _ = x_prompt.shape
    nbs, ts, _ = x_sample.shape
    p_len = cache_a_k.shape[2]
    assert ts == GRID_W * GRID_W and nbs + 1 <= 8

    w_in_bf = w_in.astype(BF16)
    w_out_bf = w_out.astype(BF16)
    w_pool_bf = w_pool.astype(BF16)
    w_pw2_bf = w_pw2.astype(BF16)
    cak = cache_a_k.reshape(nbs, DEPTH, p_len, W_A)
    cav = cache_a_v.reshape(nbs, DEPTH, p_len, W_A)
    cbk = cache_b_k.reshape(nbs, DEPTH, p_len, KV_B * HEAD_DIM)
    cbv = cache_b_v.reshape(nbs, DEPTH, p_len, KV_B * HEAD_DIM)

    cond8 = jnp.concatenate([c_ctx[None, :], c, jnp.zeros((8 - 1 - nbs, D_MODEL), F32)], axis=0)
    ada = _ada(cond8, w_ada, b_ada)
    rope = _rope_tables(ts)
    band_mask = _band_mask(ts)

    def mod(l, k, lo, hi):
        return ada[l, lo:hi, k * D_MODEL:(k + 1) * D_MODEL].reshape(hi - lo, 1, D_MODEL)

    def next_norm(l, lo, hi):
        if l == DEPTH - 1:
            return (final_g,)
        return (norm_g[l + 1], mod(l + 1, 0, lo, hi), mod(l + 1, 1, lo, hi))

    xp = x_prompt.reshape(nbp * tp, D_MODEL)
    xs = x_sample.reshape(nbs * ts, D_MODEL)
    hp = _prenorm(xp, norm_g[0], mod(0, 0, 0, 1), mod(0, 1, 0, 1), nbp * tp)
    hs = _prenorm(xs, norm_g[0], mod(0, 0, 1, 1 + nbs), mod(0, 1, 1, 1 + nbs), ts)
    kv_out = [[], [], [], []]
    for l in range(DEPTH):
        na_bias, na_mask = _na_tables(rpb_a[l])

        proj = _inproj(hp, w_in_bf, l)
        ya, yb, nak, nav, nbk, nbv = _ctx_attn(proj, sink_b[l], nbp, tp)
        ycd = _cd_mixers(proj, w_pool_bf[l], pool_scale[l], conv_w[l], conv_b[l], ln_g[l], ln_b[l],
                         w_pw2_bf[l], b_pw2[l], nbp, tp, tt=min(tp, CD_TT))
        res = _outproj(ya, yb, ycd, w_out_bf, xp, mod(l, 2, 0, 1), next_norm(l, 0, 1), l, nbp * tp)
        if l == DEPTH - 1:
            y_prompt = res.reshape(nbp, tp, D_MODEL)
        else:
            xp, hp = res
        for lst, a in zip(kv_out, (nak, nav, nbk, nbv)):
            lst.append(a)

        proj = _inproj(hs, w_in_bf, l)
        ya = _na_attn(proj, cak, cav, na_bias, na_mask, l, nbs, ts)
        yb = _band_attn(proj, sink_b[l], cbk, cbv, rope, band_mask, l, nbs, ts)
        ycd = _cd_mixers(proj, w_pool_bf[l], pool_scale[l], conv_w[l], conv_b[l], ln_g[l], ln_b[l],
                         w_pw2_bf[l], b_pw2[l], nbs, ts, tt=min(ts, CD_TT))
        res = _outproj(ya, yb, ycd, w_out_bf, xs, mod(l, 2, 1, 1 + nbs), next_norm(l, 1, 1 + nbs), l, ts)
        if l == DEPTH - 1:
            y_sample = res.reshape(nbs, ts, D_MODEL)
        else:
            xs, hs = res

    new_a_k = jnp.stack(kv_out[0], axis=1).reshape(nbp, DEPTH, tp, H_A, HEAD_DIM)
    new_a_v = jnp.stack(kv_out[1], axis=1).reshape(nbp, DEPTH, tp, H_A, HEAD_DIM)
    new_b_k = jnp.stack(kv_out[2], axis=1).reshape(nbp, DEPTH, tp, KV_B, HEAD_DIM)
    new_b_v = jnp.stack(kv_out[3], axis=1).reshape(nbp, DEPTH, tp, KV_B, HEAD_DIM)
    return (y_prompt, y_sample, new_a_k, new_a_v, new_b_k, new_b_v)
```

```python
import functools

import numpy as np
import jax
import jax.numpy as jnp
from jax import lax
from jax.experimental import pallas as pl
from jax.experimental.pallas import tpu as pltpu

D_MODEL = 4096
DEPTH = 2
GRID_W = 64
HEAD_DIM = 128
W_A = D_MODEL // 4
W_B = D_MODEL // 4
W_C = D_MODEL // 4
W_D = D_MODEL - W_A - W_B - W_C
H_A = W_A // HEAD_DIM
H_B = W_B // HEAD_DIM
KV_B = max(1, H_B // 4)
G_B = H_B // KV_B
NA_ROWS = 8
NA_COLS = 16
BAND = 128
N_POOL = 4
POOL_WINDOWS = (2, 4, 8, 16)
POOL_GW = W_C // N_POOL
CONV_K = 31
ROPE_BASE = 10000.0
EPS = 1e-6
NEG = -1e30
LOG2E = float(np.log2(np.e))
QK_SCALE_LOG2 = HEAD_DIM ** -0.5 * LOG2E
IN_W = 4 * W_A + 2 * W_B + 2 * KV_B * HEAD_DIM + 2 * W_C + 3 * W_D

LANE = 128
N_CHUNK = IN_W // LANE
CPG = W_A // LANE
G_QA, G_KA, G_VA, G_ZA, G_QB, G_ZB, G_UC, G_ZC, G_PA, G_PB, G_ZD = range(11)
C_KB = 11 * CPG
C_VB = C_KB + KV_B
HALO = 16
NA_QR = 4
NA_KR = 12
VMEM_LIMIT = 56 * 1024 * 1024
ADA_TN = 512
PRENORM_TM = 256
INPROJ_TM, INPROJ_TN = 2048, 512
OUTPROJ_TM, OUTPROJ_TN = 512, 1024
CD_TT = 512
BAND_TQ = 1024

F32 = jnp.float32
BF16 = jnp.bfloat16


def _silu(x):
    return x * jax.nn.sigmoid(x)


def _dot(a, b):
    return jnp.dot(a, b, preferred_element_type=F32)


def _dot_nt(a, b):
    return lax.dot_general(a, b, (((1,), (1,)), ((), ())), preferred_element_type=F32)


def _params(sem):
    return pltpu.CompilerParams(dimension_semantics=sem, vmem_limit_bytes=VMEM_LIMIT)


def _ada_kernel(c_ref, w_ref, b_ref, o_ref):
    s = _silu(c_ref[...])
    o_ref[...] = _dot(s.astype(BF16), w_ref[...].astype(BF16)) + b_ref[...]


def _ada(cond8, w_ada, b_ada):
    tn = ADA_TN
    n = 3 * D_MODEL
    return pl.pallas_call(
        _ada_kernel,
        grid=(DEPTH, n // tn),
        in_specs=[
            pl.BlockSpec((8, D_MODEL), lambda l, j: (0, 0)),
            pl.BlockSpec((None, D_MODEL, tn), lambda l, j: (l, 0, j)),
            pl.BlockSpec((None, 1, tn), lambda l, j: (l, 0, j)),
        ],
        out_specs=pl.BlockSpec((None, 8, tn), lambda l, j: (l, 0, j)),
        out_shape=jax.ShapeDtypeStruct((DEPTH, 8, n), F32),
        compiler_params=_params(("arbitrary", "arbitrary")),
        name="ada",
    )(cond8, w_ada, b_ada.reshape(DEPTH, 1, n))


NORM_ROWS = 8


def _norm_tile(r):
    return pl.ds(pl.multiple_of(r * NORM_ROWS, NORM_ROWS), NORM_ROWS)


def _sumsq_sweep(src_ref, ssq_ref, n_rows):
    def stats(r, carry):
        x = src_ref[_norm_tile(r), :]
        ssq_ref[_norm_tile(r), :] = jnp.sum(x * x, axis=-1, keepdims=True)
        return carry

    lax.fori_loop(0, n_rows // NORM_ROWS, stats, 0, unroll=8)


def _normalise_sweep(src_ref, ssq_ref, n_rows, emit):
    def apply(r, carry):
        rstd = lax.rsqrt(ssq_ref[_norm_tile(r), :] * (1.0 / D_MODEL) + EPS)
        emit(_norm_tile(r), src_ref[_norm_tile(r), :] * rstd)
        return carry

    lax.fori_loop(0, n_rows // NORM_ROWS, apply, 0, unroll=8)


def _prenorm_kernel(x_ref, g_ref, sh_ref, sc_ref, h_ref, ssq_ref, *, tm):
    gs = jnp.broadcast_to(g_ref[...] * (1.0 + sc_ref[...]), (NORM_ROWS, D_MODEL))
    sh = jnp.broadcast_to(sh_ref[...], (NORM_ROWS, D_MODEL))

    def emit(rs, y):
        h_ref[rs, :] = (y * gs + sh).astype(BF16)

    _sumsq_sweep(x_ref, ssq_ref, tm)
    _normalise_sweep(x_ref, ssq_ref, tm, emit)


def _prenorm(x, norm_g, shift, scale, tokens_per_cond, tm=PRENORM_TM):
    m = x.shape[0]
    tpc = tokens_per_cond // tm
    cond = pl.BlockSpec((None, 1, D_MODEL), lambda i: (i // tpc, 0, 0))
    return pl.pallas_call(
        functools.partial(_prenorm_kernel, tm=tm),
        grid=(m // tm,),
        in_specs=[pl.BlockSpec((tm, D_MODEL), lambda i: (i, 0)), pl.BlockSpec((1, D_MODEL), lambda i: (0, 0)),
                  cond, cond],
        out_specs=pl.BlockSpec((tm, D_MODEL), lambda i: (i, 0)),
        out_shape=jax.ShapeDtypeStruct((m, D_MODEL), BF16),
        scratch_shapes=[pltpu.VMEM((tm, 1), F32)],
        compiler_params=_params(("arbitrary",)),
        name="prenorm",
    )(x, norm_g.reshape(1, D_MODEL), shift, scale)


def _inproj_kernel(h_ref, w_ref, o_ref, *, tn):
    acc = _dot(h_ref[...], w_ref[...])
    for s in range(tn // LANE):
        o_ref[s] = acc[:, s * LANE:(s + 1) * LANE]


def _inproj(h, w_bf, layer, tm=INPROJ_TM, tn=INPROJ_TN):
    m = h.shape[0]
    n_tiles = IN_W // tn
    kv0 = (4 * W_A + W_B) // tn
    n_kv = 2 * KV_B * HEAD_DIM // tn

    def src(j):
        return jnp.where(j < kv0, j, jnp.where(j < n_tiles - n_kv, j + n_kv, j - (n_tiles - n_kv) + kv0))

    return pl.pallas_call(
        functools.partial(_inproj_kernel, tn=tn),
        grid=(m // tm, n_tiles),
        in_specs=[
            pl.BlockSpec((tm, D_MODEL), lambda i, j: (i, 0)),
            pl.BlockSpec((None, D_MODEL, tn), lambda i, j: (layer, 0, src(j))),
        ],
        out_specs=pl.BlockSpec((tn // LANE, tm, LANE), lambda i, j: (j, i, 0)),
        out_shape=jax.ShapeDtypeStruct((N_CHUNK, m, LANE), F32),
        compiler_params=_params(("arbitrary", "arbitrary")),
        name="inproj",
    )(h, w_bf)


def _ctx_attn_kernel(sink_ref, qa, ka, va, za, qb, zb, kb, vb, ya, yb, nak, nav, nbk, nbv):
    for h in range(H_A):
        cols = slice(h * HEAD_DIM, (h + 1) * HEAD_DIM)
        k32 = ka[h]
        v32 = va[h]
        s = _dot_nt((qa[h] * QK_SCALE_LOG2).astype(BF16), k32.astype(BF16))
        p = jnp.exp2(s - jnp.max(s, axis=-1, keepdims=True))
        o = _dot(p.astype(BF16), v32.astype(BF16)) / jnp.sum(p, axis=-1, keepdims=True)
        ya[:, cols] = (o * _silu(za[h])).astype(BF16)
        nak[:, cols] = k32
        nav[:, cols] = v32
    for kv in range(KV_B):
        cols = slice(kv * HEAD_DIM, (kv + 1) * HEAD_DIM)
        nbk[:, cols] = kb[kv]
        nbv[:, cols] = vb[kv]
    for h in range(H_B):
        cols = slice(h * HEAD_DIM, (h + 1) * HEAD_DIM)
        kv = h // G_B
        sink = sink_ref[h] * LOG2E
        s = _dot_nt((qb[h] * QK_SCALE_LOG2).astype(BF16), kb[kv].astype(BF16))
        m = jnp.maximum(jnp.max(s, axis=-1, keepdims=True), sink)
        p = jnp.exp2(s - m)
        l = jnp.sum(p, axis=-1, keepdims=True) + jnp.exp2(sink - m)
        o = _dot(p.astype(BF16), vb[kv].astype(BF16)) / l
        yb[:, cols] = (o * _silu(zb[h])).astype(BF16)


def _ctx_attn(proj, sink, nb, t):
    m = nb * t

    def grp(g):
        return pl.BlockSpec((CPG, t, LANE), lambda b: (g, b, 0))

    def kvs(c):
        return pl.BlockSpec((KV_B, t, LANE), lambda b: (c // KV_B, b, 0))

    wide = pl.BlockSpec((t, W_A), lambda b: (b, 0))
    wide3 = pl.BlockSpec((None, t, W_A), lambda b: (b, 0, 0))
    narrow3 = pl.BlockSpec((None, t, KV_B * HEAD_DIM), lambda b: (b, 0, 0))
    return pl.pallas_call(
        _ctx_attn_kernel,
        grid=(nb,),
        in_specs=[pl.BlockSpec(memory_space=pltpu.SMEM),
                  grp(G_QA), grp(G_KA), grp(G_VA), grp(G_ZA), grp(G_QB), grp(G_ZB), kvs(C_KB), kvs(C_VB)],
        out_specs=[wide, wide, wide3, wide3, narrow3, narrow3],
        out_shape=[jax.ShapeDtypeStruct((m, W_A), BF16), jax.ShapeDtypeStruct((m, W_B), BF16),
                   jax.ShapeDtypeStruct((nb, t, W_A), F32), jax.ShapeDtypeStruct((nb, t, W_A), F32),
                   jax.ShapeDtypeStruct((nb, t, KV_B * HEAD_DIM), F32),
                   jax.ShapeDtypeStruct((nb, t, KV_B * HEAD_DIM), F32)],
        compiler_params=_params(("arbitrary",)),
        name="ctx_attn",
    )(sink, proj, proj, proj, proj, proj, proj, proj, proj)


def _na_bias_kernel(t_ref, o_ref, *, dr):
    for ty in range(dr.shape[0]):
        for qi in range(NA_QR):
            rows = slice(qi * GRID_W, (qi + 1) * GRID_W)
            for jp in range(NA_KR // 2):
                pair = jnp.concatenate([t_ref[int(dr[ty, qi, 2 * jp])], t_ref[int(dr[ty, qi, 2 * jp + 1])]], axis=1)
                o_ref[ty, rows, 2 * jp * GRID_W:(2 * jp + 2) * GRID_W] = pair


def _na_tables(rpb):
    rows = GRID_W
    n_dr = 2 * NA_ROWS - 1
    n_dc = 2 * NA_COLS - 1
    c = np.arange(GRID_W)[:, None]
    kc = np.arange(GRID_W)[None, :]
    start = np.clip(c - NA_COLS // 2, 0, GRID_W - NA_COLS)
    col_ok = (kc >= start) & (kc < start + NA_COLS)
    dc = np.clip(kc - c + NA_COLS - 1, 0, n_dc - 1)
    onehot = (dc[None] == np.arange(n_dc)[:, None, None]).astype(np.float32)
    toep = jnp.einsum("hrd,dck->hrck", rpb * LOG2E, jnp.asarray(onehot), precision=lax.Precision.HIGHEST)
    i = np.arange(NA_QR)[:, None]
    j = np.arange(NA_KR)[None, :]
    masks, drs = [], []
    for r0, ks in ((0, 0), (NA_QR, 0), (rows - NA_QR, rows - NA_KR)):
        r = r0 + i
        rs = np.clip(r - NA_ROWS // 2, 0, rows - NA_ROWS)
        kr = ks + j
        row_ok = (kr >= rs) & (kr < rs + NA_ROWS)
        masks.append(row_ok[:, None, :, None] & col_ok[None, :, None, :])
        drs.append(np.clip(kr - r + NA_ROWS - 1, 0, n_dr - 1))
    flat = (NA_QR * GRID_W, NA_KR * GRID_W)
    mask = np.stack([mk.reshape(flat) for mk in masks]).astype(np.float32)
    bias = pl.pallas_call(
        functools.partial(_na_bias_kernel, dr=np.stack(drs)),
        grid=(H_A,),
        in_specs=[pl.BlockSpec((None, n_dr, GRID_W, GRID_W), lambda h: (h, 0, 0, 0))],
        out_specs=pl.BlockSpec((None, len(drs)) + flat, lambda h: (h, 0, 0, 0)),
        out_shape=jax.ShapeDtypeStruct((H_A, len(drs)) + flat, F32),
        compiler_params=_params(("arbitrary",)),
        name="na_bias",
    )(toep)
    return bias, jnp.asarray(mask)


def _na_kernel(q_ref, k_ref, v_ref, z_ref, ck_ref, cv_ref, bias_ref, mask_ref, o_ref, *, n_blocks):
    qn = NA_QR * GRID_W
    kn = NA_KR * GRID_W
    ck = ck_ref[...].astype(BF16)
    cv = cv_ref[...].astype(BF16)

    def body(rb, carry):
        q0 = pl.multiple_of(rb * qn, qn)
        ks = jnp.clip(rb * NA_QR - NA_ROWS // 2, 0, n_blocks * NA_QR - NA_KR)
        k0 = pl.multiple_of(ks * GRID_W, GRID_W)
        ty = jnp.where(rb == 0, 0, jnp.where(rb == n_blocks - 1, 2, 1))
        q = (q_ref[pl.ds(q0, qn), :] * QK_SCALE_LOG2).astype(BF16)
        k = k_ref[pl.ds(k0, kn), :].astype(BF16)
        v = v_ref[pl.ds(k0, kn), :].astype(BF16)
        s_loc = _dot_nt(q, k) + bias_ref[ty]
        s_loc = jnp.where(mask_ref[ty] > 0.5, s_loc, NEG)
        s_ctx = _dot_nt(q, ck)
        m = jnp.maximum(jnp.max(s_loc, axis=-1, keepdims=True), jnp.max(s_ctx, axis=-1, keepdims=True))
        p_loc = jnp.exp2(s_loc - m)
        p_ctx = jnp.exp2(s_ctx - m)
        l = jnp.sum(p_loc, axis=-1, keepdims=True) + jnp.sum(p_ctx, axis=-1, keepdims=True)
        o = (_dot(p_loc.astype(BF16), v) + _dot(p_ctx.astype(BF16), cv)) / l
        o_ref[pl.ds(q0, qn), :] = (o * _silu(z_ref[pl.ds(q0, qn), :])).astype(BF16)
        return carry

    lax.fori_loop(0, n_blocks, body, 0, unroll=4)


def _na_attn(proj, cache_k, cache_v, bias, mask, layer, nb, t):
    p_len = cache_k.shape[2]
    n_blocks = t // (NA_QR * GRID_W)

    def head(g):
        return pl.BlockSpec((None, t, LANE), lambda b, h: (g * CPG + h, b, 0))

    ctx = pl.BlockSpec((None, None, p_len, HEAD_DIM), lambda b, h: (b, layer, 0, h))
    return pl.pallas_call(
        functools.partial(_na_kernel, n_blocks=n_blocks),
        grid=(nb, H_A),
        in_specs=[head(G_QA), head(G_KA), head(G_VA), head(G_ZA), ctx, ctx,
                  pl.BlockSpec((None,) + bias.shape[1:], lambda b, h: (h, 0, 0, 0)),
                  pl.BlockSpec(mask.shape, lambda b, h: (0, 0, 0))],
        out_specs=pl.BlockSpec((t, HEAD_DIM), lambda b, h: (b, h)),
        out_shape=jax.ShapeDtypeStruct((nb * t, W_A), BF16),
        compiler_params=_params(("arbitrary", "arbitrary")),
        name="na_attn",
    )(proj, proj, proj, proj, cache_k, cache_v, bias, mask)


def _rope_tables(t):
    pos = np.arange(t)
    half = HEAD_DIM // 2
    inv = 1.0 / (ROPE_BASE ** (np.arange(0, half, 2, dtype=np.float64) / half))
    ang_r = (pos // GRID_W)[:, None] * inv[None, :]
    ang_c = (pos % GRID_W)[:, None] * inv[None, :]
    zero = np.zeros_like(ang_r)
    cos = np.concatenate([np.cos(ang_r), np.cos(ang_r), np.cos(ang_c), np.cos(ang_c)], axis=1)
    s_up = np.concatenate([-np.sin(ang_r), zero, -np.sin(ang_c), zero], axis=1)
    s_dn = np.concatenate([zero, np.sin(ang_r), zero, np.sin(ang_c)], axis=1)
    return tuple(jnp.asarray(a.astype(np.float32)) for a in (cos, s_up, s_dn))


def _band_mask(t):
    row = np.tile(np.arange(BAND), G_B)[:, None]
    col = np.arange(3 * BAND)[None, :]
    out = []
    for off in (0, -BAND, -2 * BAND):
        out.append(np.abs(off + col - row) <= BAND)
    return jnp.asarray(np.stack(out).astype(np.float32))


def _rope(x, cos, s_up, s_dn):
    quarter = HEAD_DIM // 4
    return x * cos + pltpu.roll(x, HEAD_DIM - quarter, 1) * s_up + pltpu.roll(x, quarter, 1) * s_dn


def _band_kernel(sink_ref, q_ref, k_ref, v_ref, z_ref, ck_ref, cv_ref, cos_ref, up_ref, dn_ref, mask_ref,
                 o_ref, kr_ref, vr_ref, *, n_blocks, blocks_per_step):
    kv = pl.program_id(1)
    step = pl.program_id(2)
    t = n_blocks * BAND
    chunk = 512

    @pl.when(step == 0)
    def _():
        def prep(i, carry):
            r0 = pl.multiple_of(i * chunk, chunk)
            sl = pl.ds(r0, chunk)
            kr_ref[sl, :] = _rope(k_ref[sl, :], cos_ref[sl, :], up_ref[sl, :], dn_ref[sl, :]).astype(BF16)
            vr_ref[sl, :] = v_ref[sl, :].astype(BF16)
            return carry

        lax.fori_loop(0, t // chunk, prep, 0)

    ck = ck_ref[...].astype(BF16)
    cv = cv_ref[...].astype(BF16)

    def body(j, carry):
        i = step * blocks_per_step + j
        ql = pl.ds(pl.multiple_of(j * BAND, BAND), BAND)
        qs = pl.ds(pl.multiple_of(i * BAND, BAND), BAND)
        kb = jnp.clip(i - 1, 0, n_blocks - 3)
        k0 = pl.multiple_of(kb * BAND, BAND)
        ty = jnp.where(i == 0, 0, jnp.where(i == n_blocks - 1, 2, 1))
        cos = cos_ref[qs, :]
        up = up_ref[qs, :]
        dn = dn_ref[qs, :]
        q = jnp.concatenate([(_rope(q_ref[g, ql, :], cos, up, dn) * QK_SCALE_LOG2).astype(BF16)
                             for g in range(G_B)], axis=0)
        k = kr_ref[pl.ds(k0, 3 * BAND), :]
        v = vr_ref[pl.ds(k0, 3 * BAND), :]
        s_loc = jnp.where(mask_ref[ty] > 0.5, _dot_nt(q, k), NEG)
        s_ctx = _dot_nt(q, ck)
        p_loc, p_ctx, inv_l = [], [], []
        for g in range(G_B):
            hs = slice(g * BAND, (g + 1) * BAND)
            sink = sink_ref[kv * G_B + g] * LOG2E
            m = jnp.maximum(jnp.maximum(jnp.max(s_loc[hs], axis=-1, keepdims=True),
                                        jnp.max(s_ctx[hs], axis=-1, keepdims=True)), sink)
            pl_g = jnp.exp2(s_loc[hs] - m)
            pc_g = jnp.exp2(s_ctx[hs] - m)
            l = jnp.sum(pl_g, axis=-1, keepdims=True) + jnp.sum(pc_g, axis=-1, keepdims=True) + jnp.exp2(sink - m)
            p_loc.append(pl_g.astype(BF16))
            p_ctx.append(pc_g.astype(BF16))
            inv_l.append(1.0 / l)
        o = _dot(jnp.concatenate(p_loc, axis=0), v) + _dot(jnp.concatenate(p_ctx, axis=0), cv)
        for g in range(G_B):
            og = o[g * BAND:(g + 1) * BAND, :] * inv_l[g] * _silu(z_ref[g, ql, :])
            o_ref[ql, g * HEAD_DIM:(g + 1) * HEAD_DIM] = og.astype(BF16)
        return carry

    lax.fori_loop(0, blocks_per_step, body, 0, unroll=4)


def _band_attn(proj, sink, cache_k, cache_v, rope, mask, layer, nb, t, tq=BAND_TQ):
    p_len = cache_k.shape[2]
    n_blocks = t // BAND
    nq = t // tq
    cos, s_up, s_dn = rope
    heads = pl.BlockSpec((G_B, tq, LANE), lambda b, kv, s: (G_QB * CPG // G_B + kv, b * nq + s, 0))
    gates = pl.BlockSpec((G_B, tq, LANE), lambda b, kv, s: (G_ZB * CPG // G_B + kv, b * nq + s, 0))
    keys = pl.BlockSpec((None, t, LANE), lambda b, kv, s: (C_KB + kv, b, 0))
    vals = pl.BlockSpec((None, t, LANE), lambda b, kv, s: (C_VB + kv, b, 0))
    ctx = pl.BlockSpec((None, None, p_len, HEAD_DIM), lambda b, kv, s: (b, layer, 0, kv))
    tab = pl.BlockSpec((t, HEAD_DIM), lambda b, kv, s: (0, 0))
    return pl.pallas_call(
        functools.partial(_band_kernel, n_blocks=n_blocks, blocks_per_step=tq // BAND),
        grid=(nb, KV_B, nq),
        in_specs=[pl.BlockSpec(memory_space=pltpu.SMEM), heads, keys, vals, gates, ctx, ctx, tab, tab, tab,
                  pl.BlockSpec(mask.shape, lambda b, kv, s: (0, 0, 0))],
        out_specs=pl.BlockSpec((tq, G_B * HEAD_DIM), lambda b, kv, s: (b * nq + s, kv)),
        out_shape=jax.ShapeDtypeStruct((nb * t, W_B), BF16),
        scratch_shapes=[pltpu.VMEM((t, HEAD_DIM), BF16), pltpu.VMEM((t, HEAD_DIM), BF16)],
        compiler_params=_params(("arbitrary", "arbitrary", "arbitrary")),
        name="band_attn",
    )(sink, proj, proj, proj, proj, cache_k, cache_v, cos, s_up, s_dn, mask)


def _cd_kernel(uc, ucp, ucn, zc, pa, pap, pan, pb, pbp, pbn, zd, wpool, pscale, cw, cb, lng, lnb, wpw, bpw,
               o_ref, ue, he, cv, hn, *, tt, t_seq):
    i = pl.program_id(1)
    has_prev = i > 0
    has_next = i < pl.num_programs(1) - 1
    ext = tt + 2 * HALO

    for c in range(CPG):
        ue[c, 0:HALO, :] = jnp.where(has_prev, ucp[c], 0.0)
        ue[c, HALO:HALO + tt, :] = uc[c]
        ue[c, HALO + tt:ext, :] = jnp.where(has_next, ucn[c], 0.0)
        he[c, 0:HALO, :] = jnp.where(has_prev, pap[c] * jax.nn.sigmoid(pbp[c]), 0.0)
        he[c, HALO:HALO + tt, :] = pa[c] * jax.nn.sigmoid(pb[c])
        he[c, HALO + tt:ext, :] = jnp.where(has_next, pan[c] * jax.nn.sigmoid(pbn[c]), 0.0)

    rt = 64
    for r0 in range(0, tt, rt):
        edge_tile = r0 == 0 or r0 + rt == tt
        tpos = i * tt + r0 + lax.broadcasted_iota(jnp.int32, (rt, LANE), 0)
        for g, w in enumerate(POOL_WINDOWS):
            half = w // 2
            if edge_tile:
                cnt = (jnp.minimum(tpos + half, t_seq) - jnp.maximum(tpos - half, 0)).astype(F32)
            d = []
            for c in (2 * g, 2 * g + 1):
                acc = ue[c, HALO + r0 - half:HALO + r0 - half + rt, :]
                for o in range(-half + 1, half):
                    acc = acc + ue[c, HALO + r0 + o:HALO + r0 + o + rt, :]
                mean = acc / cnt if edge_tile else acc * (1.0 / w)
                d.append(mean - uc[c, r0:r0 + rt, :])
            dg = jnp.concatenate(d, axis=1).astype(BF16)
            cols = slice(g * POOL_GW, (g + 1) * POOL_GW)
            y = _dot(dg, wpool[g]) * pscale[:, cols]
            z = jnp.concatenate([zc[2 * g, r0:r0 + rt, :], zc[2 * g + 1, r0:r0 + rt, :]], axis=1)
            o_ref[r0:r0 + rt, cols] = (y * _silu(z)).astype(BF16)

    def conv_chunk(c, carry):
        wts = cw[c]
        bias = cb[c]
        for r0 in range(0, tt, rt):
            acc = jnp.zeros((rt, LANE), F32) + bias
            for k in range(CONV_K):
                off = HALO - CONV_K // 2 + k + r0
                acc = acc + he[c, off:off + rt, :] * wts[k:k + 1, :]
            cv[c, r0:r0 + rt, :] = acc
        return carry

    lax.fori_loop(0, CPG, conv_chunk, 0)

    def ln_rows(r, carry):
        r0 = pl.multiple_of(r * rt, rt)
        rs = pl.ds(r0, rt)
        xs = [cv[c, rs, :] for c in range(CPG)]
        tot = xs[0]
        for x in xs[1:]:
            tot = tot + x
        mu = jnp.sum(tot, axis=-1, keepdims=True) * (1.0 / W_D)
        sq = jnp.square(xs[0] - mu)
        for x in xs[1:]:
            sq = sq + jnp.square(x - mu)
        rstd = lax.rsqrt(jnp.sum(sq, axis=-1, keepdims=True) * (1.0 / W_D) + EPS)
        for c in range(CPG):
            cols = slice(c * LANE, (c + 1) * LANE)
            y = (xs[c] - mu) * rstd * lng[:, cols] + lnb[:, cols]
            hn[rs, cols] = _silu(y).astype(BF16)
        return carry

    lax.fori_loop(0, tt // rt, ln_rows, 0, unroll=2)

    for r0 in range(0, tt, 256):
        rs = slice(r0, r0 + 256)
        y = _dot(hn[rs, :], wpw[...]) + bpw[...]
        for c in range(CPG):
            cols = slice(c * LANE, (c + 1) * LANE)
            o_ref[rs, W_C + c * LANE:W_C + (c + 1) * LANE] = (y[:, cols] * _silu(zd[c, rs, :])).astype(BF16)


def _cd_mixers(proj, w_pool_bf, pool_scale, conv_w, conv_b, ln_g, ln_b, w_pw2_bf, b_pw2, nb, t, tt):
    m = nb * t
    nt = t // tt
    hb = tt // HALO
    last = m // HALO - 1

    def cur(g):
        return pl.BlockSpec((CPG, tt, LANE), lambda b, i: (g, b * nt + i, 0))

    def prev(g):
        return pl.BlockSpec((CPG, HALO, LANE), lambda b, i: (g, jnp.maximum((b * nt + i) * hb - 1, 0), 0))

    def nxt(g):
        return pl.BlockSpec((CPG, HALO, LANE), lambda b, i: (g, jnp.minimum((b * nt + i + 1) * hb, last), 0))

    def full(shape):
        return pl.BlockSpec(shape, lambda b, i: (0,) * len(shape))

    ext = tt + 2 * HALO
    cw = conv_w.reshape(CONV_K, CPG, LANE).transpose(1, 0, 2)
    cb = conv_b.reshape(CPG, 1, LANE)
    return pl.pallas_call(
        functools.partial(_cd_kernel, tt=tt, t_seq=t),
        grid=(nb, nt),
        in_specs=[cur(G_UC), prev(G_UC), nxt(G_UC), cur(G_ZC),
                  cur(G_PA), prev(G_PA), nxt(G_PA), cur(G_PB), prev(G_PB), nxt(G_PB), cur(G_ZD),
                  full((N_POOL, POOL_GW, POOL_GW)), full((1, W_C)), full((CPG, CONV_K, LANE)),
                  full((CPG, 1, LANE)), full((1, W_D)), full((1, W_D)), full((W_D, W_D)), full((1, W_D))],
        out_specs=pl.BlockSpec((tt, W_C + W_D), lambda b, i: (b * nt + i, 0)),
        out_shape=jax.ShapeDtypeStruct((m, W_C + W_D), BF16),
        scratch_shapes=[pltpu.VMEM((CPG, ext, LANE), F32), pltpu.VMEM((CPG, ext, LANE), F32),
                        pltpu.VMEM((CPG, tt, LANE), F32), pltpu.VMEM((tt, W_D), BF16)],
        compiler_params=_params(("arbitrary", "arbitrary")),
        name="cd_mixers",
    )(proj, proj, proj, proj, proj, proj, proj, proj, proj, proj, proj,
      w_pool_bf, pool_scale.reshape(1, W_C), cw, cb, ln_g.reshape(1, W_D), ln_b.reshape(1, W_D),
      w_pw2_bf, b_pw2.reshape(1, W_D))


def _outproj_kernel(*refs, tn, tm, last):
    if last:
        ya, yb, ycd, w_ref, x_ref, gate_ref, g_ref, y_ref, stage, ssq_ref = refs
    else:
        ya, yb, ycd, w_ref, x_ref, gate_ref, g_ref, sh_ref, sc_ref, xo_ref, h_ref, stage, ssq_ref = refs
    j = pl.program_id(1)

    @pl.when(j == 0)
    def _():
        ssq_ref[...] = jnp.zeros_like(ssq_ref)

    acc = _dot(ya[...], w_ref[0:W_A, :])
    acc = acc + _dot(yb[...], w_ref[W_A:W_A + W_B, :])
    acc = acc + _dot(ycd[...], w_ref[W_A + W_B:, :])
    x_new = x_ref[...] + gate_ref[...] * acc
    stage[:, pl.ds(pl.multiple_of(j * tn, tn), tn)] = x_new
    if not last:
        xo_ref[...] = x_new
    ssq_ref[...] += jnp.sum(x_new * x_new, axis=-1, keepdims=True)

    @pl.when(j == pl.num_programs(1) - 1)
    def _():
        if last:
            g = jnp.broadcast_to(g_ref[...], (NORM_ROWS, D_MODEL))

            def emit(rs, y):
                y_ref[rs, :] = y * g
        else:
            gs = jnp.broadcast_to(g_ref[...] * (1.0 + sc_ref[...]), (NORM_ROWS, D_MODEL))
            sh = jnp.broadcast_to(sh_ref[...], (NORM_ROWS, D_MODEL))

            def emit(rs, y):
                h_ref[rs, :] = (y * gs + sh).astype(BF16)

        _normalise_sweep(stage, ssq_ref, tm, emit)


def _outproj(ya, yb, ycd, w_bf, x, gate, norm, layer, tokens_per_cond, tm=OUTPROJ_TM, tn=OUTPROJ_TN):
    m = x.shape[0]
    tpc = tokens_per_cond // tm
    last = len(norm) == 1
    rows = pl.BlockSpec((tm, D_MODEL), lambda i, j: (i, 0))
    cond = pl.BlockSpec((None, 1, D_MODEL), lambda i, j: (i // tpc, 0, 0))
    in_specs = [
        pl.BlockSpec((tm, W_A), lambda i, j: (i, 0)),
        pl.BlockSpec((tm, W_B), lambda i, j: (i, 0)),
        pl.BlockSpec((tm, W_C + W_D), lambda i, j: (i, 0)),
        pl.BlockSpec((None, D_MODEL, tn), lambda i, j: (layer, 0, j)),
        pl.BlockSpec((tm, tn), lambda i, j: (i, j)),
        pl.BlockSpec((None, 1, tn), lambda i, j: (i // tpc, 0, j)),
        pl.BlockSpec((1, D_MODEL), lambda i, j: (0, 0)),
    ]
    args = [ya, yb, ycd, w_bf, x, gate, norm[0].reshape(1, D_MODEL)]
    scratch = [pltpu.VMEM((tm, D_MODEL), F32), pltpu.VMEM((tm, 1), F32)]
    if last:
        out_specs = rows
        out_shape = jax.ShapeDtypeStruct((m, D_MODEL), F32)
    else:
        in_specs += [cond, cond]
        args += [norm[1], norm[2]]
        out_specs = [pl.BlockSpec((tm, tn), lambda i, j: (i, j)), rows]
        out_shape = [jax.ShapeDtypeStruct((m, D_MODEL), F32), jax.ShapeDtypeStruct((m, D_MODEL), BF16)]
    return pl.pallas_call(
        functools.partial(_outproj_kernel, tn=tn, tm=tm, last=last),
        grid=(m // tm, D_MODEL // tn),
        in_specs=in_specs,
        out_specs=out_specs,
        out_shape=out_shape,
        scratch_shapes=scratch,
        compiler_params=_params(("arbitrary", "arbitrary")),
        name="outproj",
    )(*args)


def kernel(x_prompt, x_sample, cache_a_k, cache_a_v, cache_b_k, cache_b_v, c, c_ctx, norm_g, w_ada, b_ada, w_in,
           rpb_a, sink_b, w_pool, pool_scale, conv_w, conv_b, ln_g, ln_b, w_pw2, b_pw2, w_out, final_g):
    nbp, tp, _ = x_prompt.shape
    nbs, ts, _ = x_sample.shape
    p_len = cache_a_k.shape[2]
    assert ts == GRID_W * GRID_W and nbs + 1 <= 8

    w_in_bf = w_in.astype(BF16)
    w_out_bf = w_out.astype(BF16)
    w_pool_bf = w_pool.astype(BF16)
    w_pw2_bf = w_pw2.astype(BF16)
    cak = cache_a_k.reshape(nbs, DEPTH, p_len, W_A)
    cav = cache_a_v.reshape(nbs, DEPTH, p_len, W_A)
    cbk = cache_b_k.reshape(nbs, DEPTH, p_len, KV_B * HEAD_DIM)
    cbv = cache_b_v.reshape(nbs, DEPTH, p_len, KV_B * HEAD_DIM)

    cond8 = jnp.concatenate([c_ctx[None, :], c, jnp.zeros((8 - 1 - nbs, D_MODEL), F32)], axis=0)
    ada = _ada(cond8, w_ada, b_ada)
    rope = _rope_tables(ts)
    band_mask = _band_mask(ts)

    def mod(l, k, lo, hi):
        return ada[l, lo:hi, k * D_MODEL:(k + 1) * D_MODEL].reshape(hi - lo, 1, D_MODEL)

    def next_norm(l, lo, hi):
        if l == DEPTH - 1:
            return (final_g,)
        return (norm_g[l + 1], mod(l + 1, 0, lo, hi), mod(l + 1, 1, lo, hi))

    xp = x_prompt.reshape(nbp * tp, D_MODEL)
    xs = x_sample.reshape(nbs * ts, D_MODEL)
    hp = _prenorm(xp, norm_g[0], mod(0, 0, 0, 1), mod(0, 1, 0, 1), nbp * tp)
    hs = _prenorm(xs, norm_g[0], mod(0, 0, 1, 1 + nbs), mod(0, 1, 1, 1 + nbs), ts)
    kv_out = [[], [], [], []]
    for l in range(DEPTH):
        na_bias, na_mask = _na_tables(rpb_a[l])

        proj = _inproj(hp, w_in_bf, l)
        ya, yb, nak, nav, nbk, nbv = _ctx_attn(proj, sink_b[l], nbp, tp)
        ycd = _cd_mixers(proj, w_pool_bf[l], pool_scale[l], conv_w[l], conv_b[l], ln_g[l], ln_b[l],
                         w_pw2_bf[l], b_pw2[l], nbp, tp, tt=min(tp, CD_TT))
        res = _outproj(ya, yb, ycd, w_out_bf, xp, mod(l, 2, 0, 1), next_norm(l, 0, 1), l, nbp * tp)
        if l == DEPTH - 1:
            y_prompt = res.reshape(nbp, tp, D_MODEL)
        else:
            xp, hp = res
        for lst, a in zip(kv_out, (nak, nav, nbk, nbv)):
            lst.append(a)

        proj = _inproj(hs, w_in_bf, l)
        ya = _na_attn(proj, cak, cav, na_bias, na_mask, l, nbs, ts)
        yb = _band_attn(proj, sink_b[l], cbk, cbv, rope, band_mask, l, nbs, ts)
        ycd = _cd_mixers(proj, w_pool_bf[l], pool_scale[l], conv_w[l], conv_b[l], ln_g[l], ln_b[l],
                         w_pw2_bf[l], b_pw2[l], nbs, ts, tt=min(ts, CD_TT))
        res = _outproj(ya, yb, ycd, w_out_bf, xs, mod(l, 2, 1, 1 + nbs), next_norm(l, 1, 1 + nbs), l, ts)
        if l == DEPTH - 1:
            y_sample = res.reshape(nbs, ts, D_MODEL)
        else:
            xs, hs = res

    new_a_k = jnp.stack(kv_out[0], axis=1).reshape(nbp, DEPTH, tp, H_A, HEAD_DIM)
    new_a_v = jnp.stack(kv_out[1], axis=1).reshape(nbp, DEPTH, tp, H_A, HEAD_DIM)
    new_b_k = jnp.stack(kv_out[2], axis=1).reshape(nbp, DEPTH, tp, KV_B, HEAD_DIM)
    new_b_v = jnp.stack(kv_out[3], axis=1).reshape(nbp, DEPTH, tp, KV_B, HEAD_DIM)
    return (y_prompt, y_sample, new_a_k, new_a_v, new_b_k, new_b_v)
```

```python
import functools

import numpy as np
import jax
import jax.numpy as jnp
from jax import lax
from jax.experimental import pallas as pl
from jax.experimental.pallas import tpu as pltpu

D_MODEL = 4096
DEPTH = 2
GRID_W = 64
HEAD_DIM = 128
W_A = D_MODEL // 4
W_B = D_MODEL // 4
W_C = D_MODEL // 4
W_D = D_MODEL - W_A - W_B - W_C
H_A = W_A // HEAD_DIM
H_B = W_B // HEAD_DIM
KV_B = max(1, H_B // 4)
G_B = H_B // KV_B
NA_ROWS = 8
NA_COLS = 16
BAND = 128
N_POOL = 4
POOL_WINDOWS = (2, 4, 8, 16)
POOL_GW = W_C // N_POOL
CONV_K = 31
ROPE_BASE = 10000.0
EPS = 1e-6
NEG = -1e30
LOG2E = float(np.log2(np.e))
QK_SCALE_LOG2 = HEAD_DIM ** -0.5 * LOG2E
IN_W = 4 * W_A + 2 * W_B + 2 * KV_B * HEAD_DIM + 2 * W_C + 3 * W_D

LANE = 128
N_CHUNK = IN_W // LANE
CPG = W_A // LANE
G_QA, G_KA, G_VA, G_ZA, G_QB, G_ZB, G_UC, G_ZC, G_PA, G_PB, G_ZD = range(11)
C_KB = 11 * CPG
C_VB = C_KB + KV_B
HALO = 16
NA_QR = 4
NA_KR = 12
VMEM_LIMIT = 56 * 1024 * 1024
ADA_TN = 512
PRENORM_TM = 256
INPROJ_TM, INPROJ_TN = 2048, 512
OUTPROJ_TM, OUTPROJ_TN = 512, 1024
CD_TT = 512
BAND_TQ = 1024

F32 = jnp.float32
BF16 = jnp.bfloat16


def _silu(x):
    h = 0.5 * x
    return h + h * jnp.tanh(h)


def _dot(a, b):
    return jnp.dot(a, b, preferred_element_type=F32)


def _dot_nt(a, b):
    return lax.dot_general(a, b, (((1,), (1,)), ((), ())), preferred_element_type=F32)


def _params(sem):
    return pltpu.CompilerParams(dimension_semantics=sem, vmem_limit_bytes=VMEM_LIMIT)


def _ada_kernel(c_ref, w_ref, b_ref, o_ref):
    s = _silu(c_ref[...])
    o_ref[...] = _dot(s.astype(BF16), w_ref[...].astype(BF16)) + b_ref[...]


def _ada(cond8, w_ada, b_ada):
    tn = ADA_TN
    n = 3 * D_MODEL
    return pl.pallas_call(
        _ada_kernel,
        grid=(DEPTH, n // tn),
        in_specs=[
            pl.BlockSpec((8, D_MODEL), lambda l, j: (0, 0)),
            pl.BlockSpec((None, D_MODEL, tn), lambda l, j: (l, 0, j)),
            pl.BlockSpec((None, 1, tn), lambda l, j: (l, 0, j)),
        ],
        out_specs=pl.BlockSpec((None, 8, tn), lambda l, j: (l, 0, j)),
        out_shape=jax.ShapeDtypeStruct((DEPTH, 8, n), F32),
        compiler_params=_params(("arbitrary", "arbitrary")),
        name="ada",
    )(cond8, w_ada, b_ada.reshape(DEPTH, 1, n))


NORM_ROWS = 8


def _norm_tile(r):
    return pl.ds(pl.multiple_of(r * NORM_ROWS, NORM_ROWS), NORM_ROWS)


def _sumsq_sweep(src_ref, ssq_ref, n_rows):
    def stats(r, carry):
        x = src_ref[_norm_tile(r), :]
        ssq_ref[_norm_tile(r), :] = jnp.sum(x * x, axis=-1, keepdims=True)
        return carry

    lax.fori_loop(0, n_rows // NORM_ROWS, stats, 0, unroll=8)


def _normalise_sweep(src_ref, ssq_ref, n_rows, emit):
    def apply(r, carry):
        rstd = lax.rsqrt(ssq_ref[_norm_tile(r), :] * (1.0 / D_MODEL) + EPS)
        emit(_norm_tile(r), src_ref[_norm_tile(r), :] * rstd)
        return carry

    lax.fori_loop(0, n_rows // NORM_ROWS, apply, 0, unroll=8)


def _prenorm_kernel(x_ref, g_ref, sh_ref, sc_ref, h_ref, ssq_ref, *, tm):
    gs = jnp.broadcast_to(g_ref[...] * (1.0 + sc_ref[...]), (NORM_ROWS, D_MODEL))
    sh = jnp.broadcast_to(sh_ref[...], (NORM_ROWS, D_MODEL))

    def emit(rs, y):
        h_ref[rs, :] = (y * gs + sh).astype(BF16)

    _sumsq_sweep(x_ref, ssq_ref, tm)
    _normalise_sweep(x_ref, ssq_ref, tm, emit)


def _prenorm(x, norm_g, shift, scale, tokens_per_cond, tm=PRENORM_TM):
    m = x.shape[0]
    tpc = tokens_per_cond // tm
    cond = pl.BlockSpec((None, 1, D_MODEL), lambda i: (i // tpc, 0, 0))
    return pl.pallas_call(
        functools.partial(_prenorm_kernel, tm=tm),
        grid=(m // tm,),
        in_specs=[pl.BlockSpec((tm, D_MODEL), lambda i: (i, 0)), pl.BlockSpec((1, D_MODEL), lambda i: (0, 0)),
                  cond, cond],
        out_specs=pl.BlockSpec((tm, D_MODEL), lambda i: (i, 0)),
        out_shape=jax.ShapeDtypeStruct((m, D_MODEL), BF16),
        scratch_shapes=[pltpu.VMEM((tm, 1), F32)],
        compiler_params=_params(("arbitrary",)),
        name="prenorm",
    )(x, norm_g.reshape(1, D_MODEL), shift, scale)


def _inproj_kernel(h_ref, w_ref, o_ref, *, tn):
    acc = _dot(h_ref[...], w_ref[...])
    for s in range(tn // LANE):
        o_ref[s] = acc[:, s * LANE:(s + 1) * LANE]


def _inproj(h, w_bf, layer, tm=INPROJ_TM, tn=INPROJ_TN):
    m = h.shape[0]
    n_tiles = IN_W // tn
    kv0 = (4 * W_A + W_B) // tn
    n_kv = 2 * KV_B * HEAD_DIM // tn

    def src(j):
        return jnp.where(j < kv0, j, jnp.where(j < n_tiles - n_kv, j + n_kv, j - (n_tiles - n_kv) + kv0))

    return pl.pallas_call(
        functools.partial(_inproj_kernel, tn=tn),
        grid=(m // tm, n_tiles),
        in_specs=[
            pl.BlockSpec((tm, D_MODEL), lambda i, j: (i, 0)),
            pl.BlockSpec((None, D_MODEL, tn), lambda i, j: (layer, 0, src(j))),
        ],
        out_specs=pl.BlockSpec((tn // LANE, tm, LANE), lambda i, j: (j, i, 0)),
        out_shape=jax.ShapeDtypeStruct((N_CHUNK, m, LANE), F32),
        compiler_params=_params(("arbitrary", "arbitrary")),
        name="inproj",
    )(h, w_bf)


def _ctx_attn_kernel(sink_ref, qa, ka, va, za, qb, zb, kb, vb, ya, yb, nak, nav, nbk, nbv):
    for h in range(H_A):
        cols = slice(h * HEAD_DIM, (h + 1) * HEAD_DIM)
        k32 = ka[h]
        v32 = va[h]
        s = _dot_nt((qa[h] * QK_SCALE_LOG2).astype(BF16), k32.astype(BF16))
        p = jnp.exp2(s - jnp.max(s, axis=-1, keepdims=True))
        o = _dot(p.astype(BF16), v32.astype(BF16)) / jnp.sum(p, axis=-1, keepdims=True)
        ya[:, cols] = (o * _silu(za[h])).astype(BF16)
        nak[:, cols] = k32
        nav[:, cols] = v32
    for kv in range(KV_B):
        cols = slice(kv * HEAD_DIM, (kv + 1) * HEAD_DIM)
        nbk[:, cols] = kb[kv]
        nbv[:, cols] = vb[kv]
    for h in range(H_B):
        cols = slice(h * HEAD_DIM, (h + 1) * HEAD_DIM)
        kv = h // G_B
        sink = sink_ref[h] * LOG2E
        s = _dot_nt((qb[h] * QK_SCALE_LOG2).astype(BF16), kb[kv].astype(BF16))
        m = jnp.maximum(jnp.max(s, axis=-1, keepdims=True), sink)
        p = jnp.exp2(s - m)
        l = jnp.sum(p, axis=-1, keepdims=True) + jnp.exp2(sink - m)
        o = _dot(p.astype(BF16), vb[kv].astype(BF16)) / l
        yb[:, cols] = (o * _silu(zb[h])).astype(BF16)


def _ctx_attn(proj, sink, nb, t):
    m = nb * t

    def grp(g):
        return pl.BlockSpec((CPG, t, LANE), lambda b: (g, b, 0))

    def kvs(c):
        return pl.BlockSpec((KV_B, t, LANE), lambda b: (c // KV_B, b, 0))

    wide = pl.BlockSpec((t, W_A), lambda b: (b, 0))
    wide3 = pl.BlockSpec((None, t, W_A), lambda b: (b, 0, 0))
    narrow3 = pl.BlockSpec((None, t, KV_B * HEAD_DIM), lambda b: (b, 0, 0))
    return pl.pallas_call(
        _ctx_attn_kernel,
        grid=(nb,),
        in_specs=[pl.BlockSpec(memory_space=pltpu.SMEM),
                  grp(G_QA), grp(G_KA), grp(G_VA), grp(G_ZA), grp(G_QB), grp(G_ZB), kvs(C_KB), kvs(C_VB)],
        out_specs=[wide, wide, wide3, wide3, narrow3, narrow3],
        out_shape=[jax.ShapeDtypeStruct((m, W_A), BF16), jax.ShapeDtypeStruct((m, W_B), BF16),
                   jax.ShapeDtypeStruct((nb, t, W_A), F32), jax.ShapeDtypeStruct((nb, t, W_A), F32),
                   jax.ShapeDtypeStruct((nb, t, KV_B * HEAD_DIM), F32),
                   jax.ShapeDtypeStruct((nb, t, KV_B * HEAD_DIM), F32)],
        compiler_params=_params(("arbitrary",)),
        name="ctx_attn",
    )(sink, proj, proj, proj, proj, proj, proj, proj, proj)


def _na_bias_kernel(t_ref, o_ref, *, dr):
    for ty in range(dr.shape[0]):
        for qi in range(NA_QR):
            rows = slice(qi * GRID_W, (qi + 1) * GRID_W)
            for jp in range(NA_KR // 2):
                pair = jnp.concatenate([t_ref[int(dr[ty, qi, 2 * jp])], t_ref[int(dr[ty, qi, 2 * jp + 1])]], axis=1)
                o_ref[ty, rows, 2 * jp * GRID_W:(2 * jp + 2) * GRID_W] = pair


def _na_tables(rpb):
    rows = GRID_W
    n_dr = 2 * NA_ROWS - 1
    n_dc = 2 * NA_COLS - 1
    c = np.arange(GRID_W)[:, None]
    kc = np.arange(GRID_W)[None, :]
    start = np.clip(c - NA_COLS // 2, 0, GRID_W - NA_COLS)
    col_ok = (kc >= start) & (kc < start + NA_COLS)
    dc = np.clip(kc - c + NA_COLS - 1, 0, n_dc - 1)
    onehot = (dc[None] == np.arange(n_dc)[:, None, None]).astype(np.float32)
    toep = jnp.einsum("hrd,dck->hrck", rpb * LOG2E, jnp.asarray(onehot), precision=lax.Precision.HIGHEST)
    i = np.arange(NA_QR)[:, None]
    j = np.arange(NA_KR)[None, :]
    masks, drs = [], []
    for r0, ks in ((0, 0), (NA_QR, 0), (rows - NA_QR, rows - NA_KR)):
        r = r0 + i
        rs = np.clip(r - NA_ROWS // 2, 0, rows - NA_ROWS)
        kr = ks + j
        row_ok = (kr >= rs) & (kr < rs + NA_ROWS)
        masks.append(row_ok[:, None, :, None] & col_ok[None, :, None, :])
        drs.append(np.clip(kr - r + NA_ROWS - 1, 0, n_dr - 1))
    flat = (NA_QR * GRID_W, NA_KR * GRID_W)
    mask = np.stack([mk.reshape(flat) for mk in masks]).astype(np.float32)
    bias = pl.pallas_call(
        functools.partial(_na_bias_kernel, dr=np.stack(drs)),
        grid=(H_A,),
        in_specs=[pl.BlockSpec((None, n_dr, GRID_W, GRID_W), lambda h: (h, 0, 0, 0))],
        out_specs=pl.BlockSpec((None, len(drs)) + flat, lambda h: (h, 0, 0, 0)),
        out_shape=jax.ShapeDtypeStruct((H_A, len(drs)) + flat, F32),
        compiler_params=_params(("arbitrary",)),
        name="na_bias",
    )(toep)
    return bias, jnp.asarray(mask)


def _na_kernel(q_ref, k_ref, v_ref, z_ref, ck_ref, cv_ref, bias_ref, mask_ref, o_ref, *, n_blocks):
    qn = NA_QR * GRID_W
    kn = NA_KR * GRID_W
    ck = ck_ref[...].astype(BF16)
    cv = cv_ref[...].astype(BF16)

    def body(rb, carry):
        q0 = pl.multiple_of(rb * qn, qn)
        ks = jnp.clip(rb * NA_QR - NA_ROWS // 2, 0, n_blocks * NA_QR - NA_KR)
        k0 = pl.multiple_of(ks * GRID_W, GRID_W)
        ty = jnp.where(rb == 0, 0, jnp.where(rb == n_blocks - 1, 2, 1))
        q = (q_ref[pl.ds(q0, qn), :] * QK_SCALE_LOG2).astype(BF16)
        k = k_ref[pl.ds(k0, kn), :].astype(BF16)
        v = v_ref[pl.ds(k0, kn), :].astype(BF16)
        s_loc = _dot_nt(q, k) + bias_ref[ty]
        s_loc = jnp.where(mask_ref[ty] > 0.5, s_loc, NEG)
        s_ctx = _dot_nt(q, ck)
        m = jnp.maximum(jnp.max(s_loc, axis=-1, keepdims=True), jnp.max(s_ctx, axis=-1, keepdims=True))
        p_loc = jnp.exp2(s_loc - m)
        p_ctx = jnp.exp2(s_ctx - m)
        l = jnp.sum(p_loc, axis=-1, keepdims=True) + jnp.sum(p_ctx, axis=-1, keepdims=True)
        o = (_dot(p_loc.astype(BF16), v) + _dot(p_ctx.astype(BF16), cv)) / l
        o_ref[pl.ds(q0, qn), :] = (o * _silu(z_ref[pl.ds(q0, qn), :])).astype(BF16)
        return carry

    lax.fori_loop(0, n_blocks, body, 0, unroll=4)


def _na_attn(proj, cache_k, cache_v, bias, mask, layer, nb, t):
    p_len = cache_k.shape[2]
    n_blocks = t // (NA_QR * GRID_W)

    def head(g):
        return pl.BlockSpec((None, t, LANE), lambda b, h: (g * CPG + h, b, 0))

    ctx = pl.BlockSpec((None, None, p_len, HEAD_DIM), lambda b, h: (b, layer, 0, h))
    return pl.pallas_call(
        functools.partial(_na_kernel, n_blocks=n_blocks),
        grid=(nb, H_A),
        in_specs=[head(G_QA), head(G_KA), head(G_VA), head(G_ZA), ctx, ctx,
                  pl.BlockSpec((None,) + bias.shape[1:], lambda b, h: (h, 0, 0, 0)),
                  pl.BlockSpec(mask.shape, lambda b, h: (0, 0, 0))],
        out_specs=pl.BlockSpec((t, HEAD_DIM), lambda b, h: (b, h)),
        out_shape=jax.ShapeDtypeStruct((nb * t, W_A), BF16),
        compiler_params=_params(("arbitrary", "arbitrary")),
        name="na_attn",
    )(proj, proj, proj, proj, cache_k, cache_v, bias, mask)


def _rope_tables(t):
    pos = np.arange(t)
    half = HEAD_DIM // 2
    inv = 1.0 / (ROPE_BASE ** (np.arange(0, half, 2, dtype=np.float64) / half))
    ang_r = (pos // GRID_W)[:, None] * inv[None, :]
    ang_c = (pos % GRID_W)[:, None] * inv[None, :]
    zero = np.zeros_like(ang_r)
    cos = np.concatenate([np.cos(ang_r), np.cos(ang_r), np.cos(ang_c), np.cos(ang_c)], axis=1)
    s_up = np.concatenate([-np.sin(ang_r), zero, -np.sin(ang_c), zero], axis=1)
    s_dn = np.concatenate([zero, np.sin(ang_r), zero, np.sin(ang_c)], axis=1)
    return tuple(jnp.asarray(a.astype(np.float32)) for a in (cos, s_up, s_dn))


def _band_mask(t):
    row = np.tile(np.arange(BAND), G_B)[:, None]
    col = np.arange(3 * BAND)[None, :]
    out = []
    for off in (0, -BAND, -2 * BAND):
        out.append(np.abs(off + col - row) <= BAND)
    return jnp.asarray(np.stack(out).astype(np.float32))


def _rope(x, cos, s_up, s_dn):
    quarter = HEAD_DIM // 4
    return x * cos + pltpu.roll(x, HEAD_DIM - quarter, 1) * s_up + pltpu.roll(x, quarter, 1) * s_dn


def _band_kernel(sink_ref, q_ref, k_ref, v_ref, z_ref, ck_ref, cv_ref, cos_ref, up_ref, dn_ref, mask_ref,
                 o_ref, kr_ref, vr_ref, *, n_blocks, blocks_per_step):
    kv = pl.program_id(1)
    step = pl.program_id(2)
    t = n_blocks * BAND
    chunk = 512

    @pl.when(step == 0)
    def _():
        def prep(i, carry):
            r0 = pl.multiple_of(i * chunk, chunk)
            sl = pl.ds(r0, chunk)
            kr_ref[sl, :] = _rope(k_ref[sl, :], cos_ref[sl, :], up_ref[sl, :], dn_ref[sl, :]).astype(BF16)
            vr_ref[sl, :] = v_ref[sl, :].astype(BF16)
            return carry

        lax.fori_loop(0, t // chunk, prep, 0)

    ck = ck_ref[...].astype(BF16)
    cv = cv_ref[...].astype(BF16)

    def body(j, carry):
        i = step * blocks_per_step + j
        ql = pl.ds(pl.multiple_of(j * BAND, BAND), BAND)
        qs = pl.ds(pl.multiple_of(i * BAND, BAND), BAND)
        kb = jnp.clip(i - 1, 0, n_blocks - 3)
        k0 = pl.multiple_of(kb * BAND, BAND)
        ty = jnp.where(i == 0, 0, jnp.where(i == n_blocks - 1, 2, 1))
        cos = cos_ref[qs, :]
        up = up_ref[qs, :]
        dn = dn_ref[qs, :]
        q = jnp.concatenate([(_rope(q_ref[g, ql, :], cos, up, dn) * QK_SCALE_LOG2).astype(BF16)
                             for g in range(G_B)], axis=0)
        k = kr_ref[pl.ds(k0, 3 * BAND), :]
        v = vr_ref[pl.ds(k0, 3 * BAND), :]
        s_loc = jnp.where(mask_ref[ty] > 0.5, _dot_nt(q, k), NEG)
        s_ctx = _dot_nt(q, ck)
        p_loc, p_ctx, inv_l = [], [], []
        for g in range(G_B):
            hs = slice(g * BAND, (g + 1) * BAND)
            sink = sink_ref[kv * G_B + g] * LOG2E
            m = jnp.maximum(jnp.maximum(jnp.max(s_loc[hs], axis=-1, keepdims=True),
                                        jnp.max(s_ctx[hs], axis=-1, keepdims=True)), sink)
            pl_g = jnp.exp2(s_loc[hs] - m)
            pc_g = jnp.exp2(s_ctx[hs] - m)
            l = jnp.sum(pl_g, axis=-1, keepdims=True) + jnp.sum(pc_g, axis=-1, keepdims=True) + jnp.exp2(sink - m)
            p_loc.append(pl_g.astype(BF16))
            p_ctx.append(pc_g.astype(BF16))
            inv_l.append(1.0 / l)
        o = _dot(jnp.concatenate(p_loc, axis=0), v) + _dot(jnp.concatenate(p_ctx, axis=0), cv)
        for g in range(G_B):
            og = o[g * BAND:(g + 1) * BAND, :] * inv_l[g] * _silu(z_ref[g, ql, :])
            o_ref[ql, g * HEAD_DIM:(g + 1) * HEAD_DIM] = og.astype(BF16)
        return carry

    lax.fori_loop(0, blocks_per_step, body, 0, unroll=4)


def _band_attn(proj, sink, cache_k, cache_v, rope, mask, layer, nb, t, tq=BAND_TQ):
    p_len = cache_k.shape[2]
    n_blocks = t // BAND
    nq = t // tq
    cos, s_up, s_dn = rope
    heads = pl.BlockSpec((G_B, tq, LANE), lambda b, kv, s: (G_QB * CPG // G_B + kv, b * nq + s, 0))
    gates = pl.BlockSpec((G_B, tq, LANE), lambda b, kv, s: (G_ZB * CPG // G_B + kv, b * nq + s, 0))
    keys = pl.BlockSpec((None, t, LANE), lambda b, kv, s: (C_KB + kv, b, 0))
    vals = pl.BlockSpec((None, t, LANE), lambda b, kv, s: (C_VB + kv, b, 0))
    ctx = pl.BlockSpec((None, None, p_len, HEAD_DIM), lambda b, kv, s: (b, layer, 0, kv))
    tab = pl.BlockSpec((t, HEAD_DIM), lambda b, kv, s: (0, 0))
    return pl.pallas_call(
        functools.partial(_band_kernel, n_blocks=n_blocks, blocks_per_step=tq // BAND),
        grid=(nb, KV_B, nq),
        in_specs=[pl.BlockSpec(memory_space=pltpu.SMEM), heads, keys, vals, gates, ctx, ctx, tab, tab, tab,
                  pl.BlockSpec(mask.shape, lambda b, kv, s: (0, 0, 0))],
        out_specs=pl.BlockSpec((tq, G_B * HEAD_DIM), lambda b, kv, s: (b * nq + s, kv)),
        out_shape=jax.ShapeDtypeStruct((nb * t, W_B), BF16),
        scratch_shapes=[pltpu.VMEM((t, HEAD_DIM), BF16), pltpu.VMEM((t, HEAD_DIM), BF16)],
        compiler_params=_params(("arbitrary", "arbitrary", "arbitrary")),
        name="band_attn",
    )(sink, proj, proj, proj, proj, cache_k, cache_v, cos, s_up, s_dn, mask)


def _cd_kernel(uc, ucp, ucn, zc, pa, pap, pan, pb, pbp, pbn, zd, wpool, pscale, cw, cb, lng, lnb, wpw, bpw,
               o_ref, ue, he, cv, hn, *, tt, t_seq):
    i = pl.program_id(1)
    has_prev = i > 0
    has_next = i < pl.num_programs(1) - 1
    ext = tt + 2 * HALO

    for c in range(CPG):
        ue[c, 0:HALO, :] = jnp.where(has_prev, ucp[c], 0.0)
        ue[c, HALO:HALO + tt, :] = uc[c]
        ue[c, HALO + tt:ext, :] = jnp.where(has_next, ucn[c], 0.0)
        he[c, 0:HALO, :] = jnp.where(has_prev, pap[c] * jax.nn.sigmoid(pbp[c]), 0.0)
        he[c, HALO:HALO + tt, :] = pa[c] * (0.5 + 0.5 * jnp.tanh(0.5 * pb[c]))
        he[c, HALO + tt:ext, :] = jnp.where(has_next, pan[c] * jax.nn.sigmoid(pbn[c]), 0.0)

    rt = 64
    for r0 in range(0, tt, rt):
        edge_tile = r0 == 0 or r0 + rt == tt
        tpos = i * tt + r0 + lax.broadcasted_iota(jnp.int32, (rt, LANE), 0)
        for g, w in enumerate(POOL_WINDOWS):
            half = w // 2
            if edge_tile:
                cnt = (jnp.minimum(tpos + half, t_seq) - jnp.maximum(tpos - half, 0)).astype(F32)
            d = []
            for c in (2 * g, 2 * g + 1):
                acc = ue[c, HALO + r0 - half:HALO + r0 - half + rt, :]
                for o in range(-half + 1, half):
                    acc = acc + ue[c, HALO + r0 + o:HALO + r0 + o + rt, :]
                mean = acc / cnt if edge_tile else acc * (1.0 / w)
                d.append(mean - uc[c, r0:r0 + rt, :])
            dg = jnp.concatenate(d, axis=1).astype(BF16)
            cols = slice(g * POOL_GW, (g + 1) * POOL_GW)
            y = _dot(dg, wpool[g]) * pscale[:, cols]
            z = jnp.concatenate([zc[2 * g, r0:r0 + rt, :], zc[2 * g + 1, r0:r0 + rt, :]], axis=1)
            o_ref[r0:r0 + rt, cols] = (y * _silu(z)).astype(BF16)

    def conv_chunk(c, carry):
        wts = cw[c]
        bias = cb[c]
        for r0 in range(0, tt, rt):
            acc = jnp.zeros((rt, LANE), F32) + bias
            for k in range(CONV_K):
                off = HALO - CONV_K // 2 + k + r0
                acc = acc + he[c, off:off + rt, :] * wts[k:k + 1, :]
            cv[c, r0:r0 + rt, :] = acc
        return carry

    lax.fori_loop(0, CPG, conv_chunk, 0)

    def ln_rows(r, carry):
        r0 = pl.multiple_of(r * rt, rt)
        rs = pl.ds(r0, rt)
        xs = [cv[c, rs, :] for c in range(CPG)]
        tot = xs[0]
        for x in xs[1:]:
            tot = tot + x
        mu = jnp.sum(tot, axis=-1, keepdims=True) * (1.0 / W_D)
        sq = jnp.square(xs[0] - mu)
        for x in xs[1:]:
            sq = sq + jnp.square(x - mu)
        rstd = lax.rsqrt(jnp.sum(sq, axis=-1, keepdims=True) * (1.0 / W_D) + EPS)
        for c in range(CPG):
            cols = slice(c * LANE, (c + 1) * LANE)
            y = (xs[c] - mu) * rstd * lng[:, cols] + lnb[:, cols]
            hn[rs, cols] = _silu(y).astype(BF16)
        return carry

    lax.fori_loop(0, tt // rt, ln_rows, 0, unroll=2)

    for r0 in range(0, tt, 256):
        rs = slice(r0, r0 + 256)
        y = _dot(hn[rs, :], wpw[...]) + bpw[...]
        for c in range(CPG):
            cols = slice(c * LANE, (c + 1) * LANE)
            o_ref[rs, W_C + c * LANE:W_C + (c + 1) * LANE] = (y[:, cols] * _silu(zd[c, rs, :])).astype(BF16)


def _cd_mixers(proj, w_pool_bf, pool_scale, conv_w, conv_b, ln_g, ln_b, w_pw2_bf, b_pw2, nb, t, tt):
    m = nb * t
    nt = t // tt
    hb = tt // HALO
    last = m // HALO - 1

    def cur(g):
        return pl.BlockSpec((CPG, tt, LANE), lambda b, i: (g, b * nt + i, 0))

    def prev(g):
        return pl.BlockSpec((CPG, HALO, LANE), lambda b, i: (g, jnp.maximum((b * nt + i) * hb - 1, 0), 0))

    def nxt(g):
        return pl.BlockSpec((CPG, HALO, LANE), lambda b, i: (g, jnp.minimum((b * nt + i + 1) * hb, last), 0))

    def full(shape):
        return pl.BlockSpec(shape, lambda b, i: (0,) * len(shape))

    ext = tt + 2 * HALO
    cw = conv_w.reshape(CONV_K, CPG, LANE).transpose(1, 0, 2)
    cb = conv_b.reshape(CPG, 1, LANE)
    return pl.pallas_call(
        functools.partial(_cd_kernel, tt=tt, t_seq=t),
        grid=(nb, nt),
        in_specs=[cur(G_UC), prev(G_UC), nxt(G_UC), cur(G_ZC),
                  cur(G_PA), prev(G_PA), nxt(G_PA), cur(G_PB), prev(G_PB), nxt(G_PB), cur(G_ZD),
                  full((N_POOL, POOL_GW, POOL_GW)), full((1, W_C)), full((CPG, CONV_K, LANE)),
                  full((CPG, 1, LANE)), full((1, W_D)), full((1, W_D)), full((W_D, W_D)), full((1, W_D))],
        out_specs=pl.BlockSpec((tt, W_C + W_D), lambda b, i: (b * nt + i, 0)),
        out_shape=jax.ShapeDtypeStruct((m, W_C + W_D), BF16),
        scratch_shapes=[pltpu.VMEM((CPG, ext, LANE), F32), pltpu.VMEM((CPG, ext, LANE), F32),
                        pltpu.VMEM((CPG, tt, LANE), F32), pltpu.VMEM((tt, W_D), BF16)],
        compiler_params=_params(("arbitrary", "arbitrary")),
        name="cd_mixers",
    )(proj, proj, proj, proj, proj, proj, proj, proj, proj, proj, proj,
      w_pool_bf, pool_scale.reshape(1, W_C), cw, cb, ln_g.reshape(1, W_D), ln_b.reshape(1, W_D),
      w_pw2_bf, b_pw2.reshape(1, W_D))


def _outproj_kernel(*refs, tn, tm, last):
    if last:
        ya, yb, ycd, w_ref, x_ref, gate_ref, g_ref, y_ref, stage, ssq_ref = refs
    else:
        ya, yb, ycd, w_ref, x_ref, gate_ref, g_ref, sh_ref, sc_ref, xo_ref, h_ref, stage, ssq_ref = refs
    j = pl.program_id(1)

    @pl.when(j == 0)
    def _():
        ssq_ref[...] = jnp.zeros_like(ssq_ref)

    acc = _dot(ya[...], w_ref[0:W_A, :])
    acc = acc + _dot(yb[...], w_ref[W_A:W_A + W_B, :])
    acc = acc + _dot(ycd[...], w_ref[W_A + W_B:, :])
    x_new = x_ref[...] + gate_ref[...] * acc
    stage[:, pl.ds(pl.multiple_of(j * tn, tn), tn)] = x_new
    if not last:
        xo_ref[...] = x_new
    ssq_ref[...] += jnp.sum(x_new * x_new, axis=-1, keepdims=True)

    @pl.when(j == pl.num_programs(1) - 1)
    def _():
        if last:
            g = jnp.broadcast_to(g_ref[...], (NORM_ROWS, D_MODEL))

            def emit(rs, y):
                y_ref[rs, :] = y * g
        else:
            gs = jnp.broadcast_to(g_ref[...] * (1.0 + sc_ref[...]), (NORM_ROWS, D_MODEL))
            sh = jnp.broadcast_to(sh_ref[...], (NORM_ROWS, D_MODEL))

            def emit(rs, y):
                h_ref[rs, :] = (y * gs + sh).astype(BF16)

        _normalise_sweep(stage, ssq_ref, tm, emit)


def _outproj(ya, yb, ycd, w_bf, x, gate, norm, layer, tokens_per_cond, tm=OUTPROJ_TM, tn=OUTPROJ_TN):
    m = x.shape[0]
    tpc = tokens_per_cond // tm
    last = len(norm) == 1
    rows = pl.BlockSpec((tm, D_MODEL), lambda i, j: (i, 0))
    cond = pl.BlockSpec((None, 1, D_MODEL), lambda i, j: (i // tpc, 0, 0))
    in_specs = [
        pl.BlockSpec((tm, W_A), lambda i, j: (i, 0)),
        pl.BlockSpec((tm, W_B), lambda i, j: (i, 0)),
        pl.BlockSpec((tm, W_C + W_D), lambda i, j: (i, 0)),
        pl.BlockSpec((None, D_MODEL, tn), lambda i, j: (layer, 0, j)),
        pl.BlockSpec((tm, tn), lambda i, j: (i, j)),
        pl.BlockSpec((None, 1, tn), lambda i, j: (i // tpc, 0, j)),
        pl.BlockSpec((1, D_MODEL), lambda i, j: (0, 0)),
    ]
    args = [ya, yb, ycd, w_bf, x, gate, norm[0].reshape(1, D_MODEL)]
    scratch = [pltpu.VMEM((tm, D_MODEL), F32), pltpu.VMEM((tm, 1), F32)]
    if last:
        out_specs = rows
        out_shape = jax.ShapeDtypeStruct((m, D_MODEL), F32)
    else:
        in_specs += [cond, cond]
        args += [norm[1], norm[2]]
        out_specs = [pl.BlockSpec((tm, tn), lambda i, j: (i, j)), rows]
        out_shape = [jax.ShapeDtypeStruct((m, D_MODEL), F32), jax.ShapeDtypeStruct((m, D_MODEL), BF16)]
    return pl.pallas_call(
        functools.partial(_outproj_kernel, tn=tn, tm=tm, last=last),
        grid=(m // tm, D_MODEL // tn),
        in_specs=in_specs,
        out_specs=out_specs,
        out_shape=out_shape,
        scratch_shapes=scratch,
        compiler_params=_params(("arbitrary", "arbitrary")),
        name="outproj",
    )(*args)


def kernel(x_prompt, x_sample, cache_a_k, cache_a_v, cache_b_k, cache_b_v, c, c_ctx, norm_g, w_ada, b_ada, w_in,
           rpb_a, sink_b, w_pool, pool_scale, conv_w, conv_b, ln_g, ln_b, w_pw2, b_pw2, w_out, final_g):
    nbp, tp, _ = x_prompt.shape
    nbs, ts, _ = x_sample.shape
    p_len = cache_a_k.shape[2]
    assert ts == GRID_W * GRID_W and nbs + 1 <= 8

    w_in_bf = w_in.astype(BF16)
    w_out_bf = w_out.astype(BF16)
    w_pool_bf = w_pool.astype(BF16)
    w_pw2_bf = w_pw2.astype(BF16)
    cak = cache_a_k.reshape(nbs, DEPTH, p_len, W_A)
    cav = cache_a_v.reshape(nbs, DEPTH, p_len, W_A)
    cbk = cache_b_k.reshape(nbs, DEPTH, p_len, KV_B * HEAD_DIM)
    cbv = cache_b_v.reshape(nbs, DEPTH, p_len, KV_B * HEAD_DIM)

    cond8 = jnp.concatenate([c_ctx[None, :], c, jnp.zeros((8 - 1 - nbs, D_MODEL), F32)], axis=0)
    ada = _ada(cond8, w_ada, b_ada)
    rope = _rope_tables(ts)
    band_mask = _band_mask(ts)

    def mod(l, k, lo, hi):
        return ada[l, lo:hi, k * D_MODEL:(k + 1) * D_MODEL].reshape(hi - lo, 1, D_MODEL)

    def next_norm(l, lo, hi):
        if l == DEPTH - 1:
            return (final_g,)
        return (norm_g[l + 1], mod(l + 1, 0, lo, hi), mod(l + 1, 1, lo, hi))

    xp = x_prompt.reshape(nbp * tp, D_MODEL)
    xs = x_sample.reshape(nbs * ts, D_MODEL)
    hp = _prenorm(xp, norm_g[0], mod(0, 0, 0, 1), mod(0, 1, 0, 1), nbp * tp)
    hs = _prenorm(xs, norm_g[0], mod(0, 0, 1, 1 + nbs), mod(0, 1, 1, 1 + nbs), ts)
    kv_out = [[], [], [], []]
    for l in range(DEPTH):
        na_bias, na_mask = _na_tables(rpb_a[l])

        proj = _inproj(hp, w_in_bf, l)
        ya, yb, nak, nav, nbk, nbv = _ctx_attn(proj, sink_b[l], nbp, tp)
        ycd = _cd_mixers(proj, w_pool_bf[l], pool_scale[l], conv_w[l], conv_b[l], ln_g[l], ln_b[l],
                         w_pw2_bf[l], b_pw2[l], nbp, tp, tt=min(tp, CD_TT))
        res = _outproj(ya, yb, ycd, w_out_bf, xp, mod(l, 2, 0, 1), next_norm(l, 0, 1), l, nbp * tp)
        if l == DEPTH - 1:
            y_prompt = res.reshape(nbp, tp, D_MODEL)
        else:
            xp, hp = res
        for lst, a in zip(kv_out, (nak, nav, nbk, nbv)):
            lst.append(a)

        proj = _inproj(hs, w_in_bf, l)
        ya = _na_attn(proj, cak, cav, na_bias, na_mask, l, nbs, ts)
        yb = _band_attn(proj, sink_b[l], cbk, cbv, rope, band_mask, l, nbs, ts)
        ycd = _cd_mixers(proj, w_pool_bf[l], pool_scale[l], conv_w[l], conv_b[l], ln_g[l], ln_b[l],
                         w_pw2_bf[l], b_pw2[l], nbs, ts, tt=min(ts, CD_TT))
        res = _outproj(ya, yb, ycd, w_out_bf, xs, mod(l, 2, 1, 1 + nbs), next_norm(l, 1, 1 + nbs), l, ts)
        if l == DEPTH - 1:
            y_sample = res.reshape(nbs, ts, D_MODEL)
        else:
            xs, hs = res

    new_a_k = jnp.stack(kv_out[0], axis=1).reshape(nbp, DEPTH, tp, H_A, HEAD_DIM)
    new_a_v = jnp.stack(kv_out[1], axis=1).reshape(nbp, DEPTH, tp, H_A, HEAD_DIM)
    new_b_k = jnp.stack(kv_out[2], axis=1).reshape(nbp, DEPTH, tp, KV_B, HEAD_DIM)
    new_b_v = jnp.stack(kv_out[3], axis=1).reshape(nbp, DEPTH, tp, KV_B, HEAD_DIM)
    return (y_prompt, y_sample, new_a_k, new_a_v, new_b_k, new_b_v)
```
